```python
import functools
import jax, jax.numpy as jnp
from jax import lax
import numpy as np

D_MODEL = 1024
BATCH = 4
SEQ = 8192
DEPTH = 1
DEC_BATCH = 8
DEC_SEQ = 16
PAST_LEN = 4096

CHUNK = 64
BAND_CHUNKS = 8
WINDOW = BAND_CHUNKS * CHUNK
ATT_HEADS = 16
ATT_HEAD_DIM = 64
ATT_WIDTH = ATT_HEADS * ATT_HEAD_DIM
MAX_REL = 256
SSD_WIDTH = 2 * D_MODEL
SSD_HEAD_DIM = 64
SSD_HEADS = SSD_WIDTH // SSD_HEAD_DIM
SSD_GROUPS = 4
SSD_STATE = 128
CONV_WIDTH = 4
CONV_CH = SSD_WIDTH + 2 * SSD_GROUPS * SSD_STATE
N_BRANCH = 2
EPS = 1e-6
IN_SIZES = (ATT_WIDTH, ATT_WIDTH, ATT_WIDTH, ATT_WIDTH, SSD_WIDTH, CONV_CH, SSD_HEADS, N_BRANCH * D_MODEL)
IN_DIM = sum(IN_SIZES)
IN_SPLITS = tuple(int(v) for v in np.cumsum(IN_SIZES)[:-1])

kernel_name = "hybrid_chunkband_ssd_streaming_step"


def rms_norm(x, g):
    xf = x.astype(jnp.float32)
    y = xf * lax.rsqrt(jnp.mean(xf * xf, axis=-1, keepdims=True) + EPS)
    return (y * g.astype(jnp.float32)).astype(x.dtype)


def rel_bias_for(q_pos, k_pos, rel_bias):
    idx = jnp.clip(q_pos[:, None] - k_pos[None, :], -MAX_REL, MAX_REL) + MAX_REL
    return rel_bias[:, idx]


def attend(q, k, v, bias, mask):
    s = jnp.einsum('bqhd,bkhd->bhqk', q, k).astype(jnp.float32) * (ATT_HEAD_DIM ** -0.5) + bias[None].astype(jnp.float32)
    s = jnp.where(mask, s, -1e30)
    p = jax.nn.softmax(s, axis=-1).astype(v.dtype)
    return jnp.einsum('bhqk,bkhd->bqhd', p, v)


def prompt_attention(q, k, v, rel_bias):
    b, s = q.shape[:2]
    nc = s // CHUNK
    band = WINDOW + CHUNK
    kp = jnp.pad(k, ((0, 0), (WINDOW, 0), (0, 0), (0, 0)))
    vp = jnp.pad(v, ((0, 0), (WINDOW, 0), (0, 0), (0, 0)))
    k_loc = jnp.arange(band)
    bias = rel_bias_for(jnp.arange(CHUNK) + WINDOW, k_loc, rel_bias)
    qc = jnp.moveaxis(q.reshape(b, nc, CHUNK, ATT_HEADS, ATT_HEAD_DIM), 1, 0)

    def one_chunk(args):
        i, qi = args
        start = i * CHUNK
        kb = lax.dynamic_slice_in_dim(kp, start, band, axis=1)
        vb = lax.dynamic_slice_in_dim(vp, start, band, axis=1)
        mask = (start - WINDOW + k_loc >= 0)[None, None, None, :]
        return attend(qi, kb, vb, bias, mask)

    o = lax.map(one_chunk, (jnp.arange(nc), qc))
    o = jnp.moveaxis(o, 0, 1).reshape(b, s, ATT_HEADS, ATT_HEAD_DIM)
    rows = min(WINDOW, s)
    return o, k[:, s - rows:], v[:, s - rows:]


def sample_attention(q, k, v, rel_bias, cache_k, cache_v):
    rows, t = cache_k.shape[1], q.shape[1]
    k_all = jnp.concatenate([cache_k.astype(k.dtype), k], axis=1)
    v_all = jnp.concatenate([cache_v.astype(v.dtype), v], axis=1)
    q_pos = PAST_LEN + jnp.arange(t)
    k_pos = PAST_LEN + jnp.arange(-rows, t)
    qc, kc = q_pos // CHUNK, k_pos // CHUNK
    mask = ((kc[None, :] <= qc[:, None]) & (kc[None, :] >= qc[:, None] - BAND_CHUNKS))[None, None]
    o = attend(q, k_all, v_all, rel_bias_for(q_pos, k_pos, rel_bias), mask)
    return o, k, v


def causal_dwconv(u, buf, w, b):
    s = u.shape[1]
    up = jnp.concatenate([buf.astype(u.dtype), u], axis=1)
    y = b
    for j in range(CONV_WIDTH):
        y = y + up[:, j:j + s] * w[j]
    return y, up[:, up.shape[1] - (CONV_WIDTH - 1):]


def ssd_scan(xs, dt, a, bm, cm, h0):
    b, s = xs.shape[:2]
    e = SSD_HEADS // SSD_GROUPS
    ln = min(CHUNK, s)
    nc = s // ln

    def to_chunks(t):
        return jnp.moveaxis(t.reshape((b, nc, ln) + t.shape[2:]), 1, 0)

    xc = to_chunks(xs.reshape(b, s, SSD_GROUPS, e, SSD_HEAD_DIM))
    dtc = to_chunks(dt.reshape(b, s, SSD_GROUPS, e))
    bc, cc = to_chunks(bm), to_chunks(cm)
    ag = a.reshape(SSD_GROUPS, e)
    causal = jnp.tril(jnp.ones((ln, ln), dtype=bool))[None, :, :, None, None]

    def step(h, inp):
        x, d, bk, ck = inp
        a_cs = jnp.cumsum(d * ag, axis=1)
        seg = a_cs[:, :, None] - a_cs[:, None, :]
        decay = jnp.exp(jnp.where(causal, seg, -jnp.inf))
        cb = jnp.einsum('blgn,bsgn->blsg', ck, bk)
        xdt = x * d[..., None]
        y_in = jnp.einsum('blsge,bsgep->blgep', cb[..., None] * decay, xdt)
        y_st = jnp.einsum('blgn,bgepn->blgep', ck, h) * jnp.exp(a_cs)[..., None]
        to_end = jnp.exp(a_cs[:, -1:] - a_cs)
        h_new = h * jnp.exp(a_cs[:, -1])[..., None, None] + jnp.einsum('blgn,blgep->bgepn', bk, xdt * to_end[..., None])
        return h_new, y_in + y_st

    h_fin, yc = lax.scan(step, h0.reshape(b, SSD_GROUPS, e, SSD_HEAD_DIM, SSD_STATE), (xc, dtc, bc, cc))
    y = jnp.moveaxis(yc, 0, 1).reshape(b, s, SSD_HEADS, SSD_HEAD_DIM)
    return y, h_fin.reshape(b, SSD_HEADS, SSD_HEAD_DIM, SSD_STATE)


def trunk_layer(x, c, conv_buf, h0, attn_fn, norm_g, w_ada, b_ada, w_in, q_norm_g, k_norm_g,
                w_att_proj, conv_w, conv_b, dt_bias, a_log, d_skip, ssd_norm_g, w_ssd_proj, w_out):
    b, s = x.shape[:2]
    mod = jax.nn.silu(c) @ w_ada + b_ada
    shift, scale, gate = jnp.split(mod[:, None, :], 3, axis=-1)
    h = rms_norm(x, norm_g) * (1 + scale) + shift
    proj = h @ w_in
    q, k, v, z_att, z_ssd, xbc, dt_raw, gates = jnp.split(proj, IN_SPLITS, axis=-1)
    q = rms_norm(q.reshape(b, s, ATT_HEADS, ATT_HEAD_DIM), q_norm_g)
    k = rms_norm(k.reshape(b, s, ATT_HEADS, ATT_HEAD_DIM), k_norm_g)
    v = v.reshape(b, s, ATT_HEADS, ATT_HEAD_DIM)
    o_att, k_st, v_st = attn_fn(q, k, v)
    att_branch = (o_att.reshape(b, s, ATT_WIDTH) * jax.nn.silu(z_att)) @ w_att_proj
    xbc, conv_new = causal_dwconv(xbc, conv_buf, conv_w, conv_b)
    xbc = jax.nn.silu(xbc)
    xs, bm, cm = jnp.split(xbc, [SSD_WIDTH, SSD_WIDTH + SSD_GROUPS * SSD_STATE], axis=-1)
    xs_h = xs.reshape(b, s, SSD_HEADS, SSD_HEAD_DIM).astype(jnp.float32)
    dt = jax.nn.softplus(dt_raw.astype(jnp.float32) + dt_bias.astype(jnp.float32))
    a = -jnp.exp(a_log.astype(jnp.float32))
    y, h_new = ssd_scan(xs_h, dt, a,
                        bm.reshape(b, s, SSD_GROUPS, SSD_STATE).astype(jnp.float32),
                        cm.reshape(b, s, SSD_GROUPS, SSD_STATE).astype(jnp.float32),
                        h0.astype(jnp.float32))
    y = (y + d_skip.astype(jnp.float32)[:, None] * xs_h).reshape(b, s, SSD_WIDTH).astype(x.dtype)
    y = y * jax.nn.silu(z_ssd)
    y = rms_norm(y.reshape(b, s, SSD_GROUPS, SSD_WIDTH // SSD_GROUPS),
                 ssd_norm_g.reshape(SSD_GROUPS, SSD_WIDTH // SSD_GROUPS)).reshape(b, s, SSD_WIDTH)
    ssd_branch = y @ w_ssd_proj
    g_att, g_ssd = jnp.split(gates, 2, axis=-1)
    merged = jax.nn.sigmoid(g_att) * att_branch + jax.nn.sigmoid(g_ssd) * ssd_branch
    out = x + gate * (merged @ w_out)
    return out, k_st, v_st, conv_new, h_new.astype(h0.dtype)


def setup_inputs(seed: int = 0) -> dict:
    key = jax.random.key(seed)
    ks = jax.random.split(key, 32)
    f32 = jnp.float32
    rows = min(WINDOW, PAST_LEN)
    nrm = lambda k, shp, sc: jax.random.normal(k, shp, f32) * sc
    dt0 = jnp.exp(jax.random.uniform(ks[20], (DEPTH, SSD_HEADS), f32, np.log(1e-3), np.log(1e-1)))
    return {
        "x_prompt": nrm(ks[0], (BATCH, SEQ, D_MODEL), 1.0),
        "x_sample": nrm(ks[1], (DEC_BATCH, DEC_SEQ, D_MODEL), 1.0),
        "c_prompt": nrm(ks[2], (BATCH, D_MODEL), 1.0),
        "c_sample": nrm(ks[3], (DEC_BATCH, D_MODEL), 1.0),
        "cache_k": nrm(ks[4], (DEPTH, DEC_BATCH, rows, ATT_HEADS, ATT_HEAD_DIM), 1.0),
        "cache_v": nrm(ks[5], (DEPTH, DEC_BATCH, rows, ATT_HEADS, ATT_HEAD_DIM), 1.0),
        "state_conv": nrm(ks[6], (DEPTH, DEC_BATCH, CONV_WIDTH - 1, CONV_CH), 1.0),
        "state_ssm": nrm(ks[7], (DEPTH, DEC_BATCH, SSD_HEADS, SSD_HEAD_DIM, SSD_STATE), 0.5),
        "norm_g": 1.0 + nrm(ks[8], (DEPTH, D_MODEL), 0.02),
        "w_ada": nrm(ks[9], (DEPTH, D_MODEL, 3 * D_MODEL), 0.5 * D_MODEL ** -0.5),
        "b_ada": nrm(ks[10], (DEPTH, 3 * D_MODEL), 0.01),
        "w_in": nrm(ks[11], (DEPTH, D_MODEL, IN_DIM), D_MODEL ** -0.5),
        "q_norm_g": 1.0 + nrm(ks[12], (DEPTH, ATT_HEAD_DIM), 0.02),
        "k_norm_g": 1.0 + nrm(ks[13], (DEPTH, ATT_HEAD_DIM), 0.02),
        "rel_bias": nrm(ks[14], (DEPTH, ATT_HEADS, 2 * MAX_REL + 1), 0.1),
        "w_att_proj": nrm(ks[15], (DEPTH, ATT_WIDTH, D_MODEL), ATT_WIDTH ** -0.5),
        "conv_w": nrm(ks[16], (DEPTH, CONV_WIDTH, CONV_CH), CONV_WIDTH ** -0.5),
        "conv_b": nrm(ks[17], (DEPTH, CONV_CH), 0.01),
        "dt_bias": dt0 + jnp.log(-jnp.expm1(-dt0)),
        "a_log": jnp.log(jax.random.uniform(ks[18], (DEPTH, SSD_HEADS), f32, 1.0, 16.0)),
        "d_skip": 1.0 + nrm(ks[19], (DEPTH, SSD_HEADS), 0.1),
        "ssd_norm_g": 1.0 + nrm(ks[21], (DEPTH, SSD_WIDTH), 0.02),
        "w_ssd_proj": nrm(ks[22], (DEPTH, SSD_WIDTH, D_MODEL), SSD_WIDTH ** -0.5),
        "w_out": nrm(ks[23], (DEPTH, D_MODEL, D_MODEL), D_MODEL ** -0.5),
    }


def reference(x_prompt, x_sample, c_prompt, c_sample, cache_k, cache_v, state_conv, state_ssm,
              norm_g, w_ada, b_ada, w_in, q_norm_g, k_norm_g, rel_bias, w_att_proj,
              conv_w, conv_b, dt_bias, a_log, d_skip, ssd_norm_g, w_ssd_proj, w_out):
    xp, xs = x_prompt, x_sample
    kp_l, vp_l, cp_l, hp_l, ks_l, vs_l, cs_l, hs_l = [], [], [], [], [], [], [], []
    for l in range(DEPTH):
        lw = (norm_g[l], w_ada[l], b_ada[l], w_in[l], q_norm_g[l], k_norm_g[l], w_att_proj[l],
              conv_w[l], conv_b[l], dt_bias[l], a_log[l], d_skip[l], ssd_norm_g[l], w_ssd_proj[l], w_out[l])
        conv0 = jnp.zeros((xp.shape[0], CONV_WIDTH - 1, CONV_CH), xp.dtype)
        h0 = jnp.zeros((xp.shape[0], SSD_HEADS, SSD_HEAD_DIM, SSD_STATE), state_ssm.dtype)
        xp, kp, vp, cp, hp = trunk_layer(xp, c_prompt, conv0, h0,
                                         functools.partial(prompt_attention, rel_bias=rel_bias[l]), *lw)
        xs, ks, vs, cs, hs = trunk_layer(xs, c_sample, state_conv[l], state_ssm[l],
                                         functools.partial(sample_attention, rel_bias=rel_bias[l],
                                                           cache_k=cache_k[l], cache_v=cache_v[l]), *lw)
        kp_l.append(kp); vp_l.append(vp); cp_l.append(cp); hp_l.append(hp)
        ks_l.append(ks); vs_l.append(vs); cs_l.append(cs); hs_l.append(hs)
    return (xp, xs, jnp.stack(kp_l), jnp.stack(vp_l), jnp.stack(cp_l), jnp.stack(hp_l),
            jnp.stack(ks_l), jnp.stack(vs_l), jnp.stack(cs_l), jnp.stack(hs_l))
```

```python
import functools

import jax
import jax.numpy as jnp
from jax import lax
from jax.experimental import pallas as pl
from jax.experimental.pallas import tpu as pltpu

F32 = jnp.float32
BF16 = jnp.bfloat16

D_MODEL = 1024
CHUNK = 64
BAND_CHUNKS = 8
WINDOW = BAND_CHUNKS * CHUNK
ATT_HEADS = 16
ATT_HEAD_DIM = 64
ATT_WIDTH = ATT_HEADS * ATT_HEAD_DIM
MAX_REL = 256
SSD_WIDTH = 2 * D_MODEL
SSD_HEAD_DIM = 64
SSD_HEADS = SSD_WIDTH // SSD_HEAD_DIM
SSD_GROUPS = 4
SSD_STATE = 128
CONV_WIDTH = 4
CONV_CH = SSD_WIDTH + 2 * SSD_GROUPS * SSD_STATE
PAST_LEN = 4096
EPS = 1e-6
NEG = -1e30

LANES = 128
SUBLANES = 8
VMEM_LIMIT = 56 * 1024 * 1024

ATT_TQ = 256
ATT_TK = ATT_TQ + WINDOW
SSD_L = 256
SAMPLE_PAD = 128
HEADS_PER_GROUP = SSD_HEADS // SSD_GROUPS
GROUP_W = SSD_WIDTH // SSD_GROUPS
HALO = SUBLANES


def _const(shape, index_map):
    return pl.BlockSpec(shape, index_map, pipeline_mode=pl.Buffered(1))


def _params(n_axes):
    return pltpu.CompilerParams(dimension_semantics=("arbitrary",) * n_axes, vmem_limit_bytes=VMEM_LIMIT)


def _dot(a, b):
    return jnp.dot(a, b, preferred_element_type=F32)


def _dot_nt(a, b):
    return lax.dot_general(a, b, (((1,), (1,)), ((), ())), preferred_element_type=F32)


def _split3(x):
    hi = x.astype(BF16)
    r1 = x - hi.astype(F32)
    mid = r1.astype(BF16)
    lo = (r1 - mid.astype(F32)).astype(BF16)
    return hi, mid, lo


def _softplus(x):
    return jnp.maximum(x, 0.0) + jnp.log1p(jnp.exp(-jnp.abs(x)))


def _mod_kernel(c_ref, w_ref, b_ref, o_ref):
    c = c_ref[...]
    a = (c * jax.nn.sigmoid(c)).astype(BF16)
    o_ref[...] = _dot(a, w_ref[...].astype(BF16)) + b_ref[...]


def _modulation(c_all, w_ada, b_ada):
    n = c_all.shape[0]
    return pl.pallas_call(
        _mod_kernel,
        out_shape=jax.ShapeDtypeStruct((n, 3 * D_MODEL), F32),
        grid=(3,),
        in_specs=[
            pl.BlockSpec((n, D_MODEL), lambda j: (0, 0)),
            pl.BlockSpec((D_MODEL, D_MODEL), lambda j: (0, j)),
            pl.BlockSpec((1, D_MODEL), lambda j: (0, j)),
        ],
        out_specs=pl.BlockSpec((n, D_MODEL), lambda j: (0, j)),
        compiler_params=_params(1),
        name="adaln_modulation",
    )(c_all, w_ada, b_ada.reshape(1, -1))


def _inproj_kernel(x_ref, shift_ref, scale_ref, ng_ref, wq, wk, wv, wza, wzs, wxbc, wg, wdt, wdtT,
                   gsum, gexp, qg, kg, q_o, k_o, v_o, za_o, zs_o, xbc_o, g_o, dt_o, dtT_o):
    x = x_ref[...]
    ms = jnp.mean(x * x, axis=-1, keepdims=True)
    h = x * lax.rsqrt(ms + EPS) * ng_ref[...]
    h = h * (1.0 + scale_ref[...]) + shift_ref[...]
    hb = h.astype(BF16)

    def head_norm(y, g_ref):
        ss = _dot((y * y).astype(BF16), gsum[...])
        r = lax.rsqrt(ss + EPS)
        r_hi = r.astype(BF16)
        r_lo = (r - r_hi.astype(F32)).astype(BF16)
        rx = _dot(r_hi, gexp[...]) + _dot(r_lo, gexp[...])
        return y * rx * g_ref[...]

    q_o[...] = head_norm(_dot(hb, wq[...]), qg).astype(q_o.dtype)
    k_o[...] = head_norm(_dot(hb, wk[...]), kg).astype(k_o.dtype)
    v_o[...] = _dot(hb, wv[...]).astype(v_o.dtype)
    za = _dot(hb, wza[...])
    za_o[...] = (za * jax.nn.sigmoid(za)).astype(za_o.dtype)
    zs = _dot(hb, wzs[...])
    zs_o[...] = (zs * jax.nn.sigmoid(zs)).astype(zs_o.dtype)
    xbc_o[...] = _dot(hb, wxbc[...]).astype(xbc_o.dtype)
    g_o[...] = jax.nn.sigmoid(_dot(hb, wg[...])).astype(g_o.dtype)
    dt_o[...] = _dot(hb, wdt[...])
    dtT_o[...] = _dot_nt(wdtT[...], hb)


def _input_projection(x, shift, scale, norm_g, wts, tm):
    b, s, _ = x.shape
    t = b * s
    per_b = s // tm
    x2 = x.reshape(t, D_MODEL)
    row = lambda i: (i, 0)
    bat = lambda i: (i // per_b, 0, 0)
    zero = lambda i: (0, 0)
    w_names = ("wq", "wk", "wv", "wza", "wzs", "wxbc", "wg", "wdt", "wdtT", "gsum", "gexp", "qg", "kg")
    w_list = [wts[n] for n in w_names]
    widths = (ATT_WIDTH, ATT_WIDTH, ATT_WIDTH, ATT_WIDTH, SSD_WIDTH, CONV_CH, 2 * D_MODEL)
    out_shape = [jax.ShapeDtypeStruct((t, w), BF16) for w in widths]
    out_shape += [jax.ShapeDtypeStruct((t, LANES), F32), jax.ShapeDtypeStruct((LANES, t), F32)]
    out_specs = [pl.BlockSpec((tm, w), row) for w in widths]
    out_specs += [pl.BlockSpec((tm, LANES), row), pl.BlockSpec((LANES, tm), lambda i: (0, i))]
    return pl.pallas_call(
        _inproj_kernel,
        out_shape=out_shape,
        grid=(t // tm,),
        in_specs=[
            pl.BlockSpec((tm, D_MODEL), row),
            pl.BlockSpec((None, 1, D_MODEL), bat),
            pl.BlockSpec((None, 1, D_MODEL), bat),
            _const((1, D_MODEL), zero),
        ] + [_const(w.shape, zero) for w in w_list],
        out_specs=out_specs,
        compiler_params=_params(1),
        name="input_projection",
    )(x2, shift, scale, norm_g.reshape(1, -1), *w_list)


def _bias_kernel(rb_ref, bp_ref, bsc_ref, bsn_ref, *, n_new, past_len):
    tq, tk = ATT_TQ, ATT_TK
    w = tq + tk
    n_tab = rb_ref.shape[1]
    v = lax.broadcasted_iota(jnp.int32, (n_tab, w), 1)
    d = lax.broadcasted_iota(jnp.int32, (n_tab, w), 0)
    idx = jnp.clip(tq - v + WINDOW, -MAX_REL, MAX_REL) + MAX_REL
    onehot = jnp.where(idx == d, 1.0, 0.0).astype(BF16)
    hi, mid, lo = _split3(rb_ref[...])
    base = _dot(hi, onehot) + _dot(mid, onehot) + _dot(lo, onehot)

    chunk_of = lambda pos: jnp.right_shift(pos, CHUNK.bit_length() - 1)
    qi = chunk_of(lax.broadcasted_iota(jnp.int32, (tq, tk), 0))
    kj = chunk_of(lax.broadcasted_iota(jnp.int32, (tq, tk), 1))
    band = (kj >= qi) & (kj <= qi + BAND_CHUNKS)
    sq = chunk_of(past_len + lax.broadcasted_iota(jnp.int32, (n_new, tk), 0))
    sk = chunk_of(past_len - WINDOW + lax.broadcasted_iota(jnp.int32, (n_new, tk), 1))
    sband = (sk <= sq) & (sk >= sq - BAND_CHUNKS)
    for h in range(ATT_HEADS):
        rows = jnp.broadcast_to(base[h:h + 1, :], (tq, w))
        toep = pltpu.roll(rows, 0, 1, stride=1, stride_axis=0)[:, tq:]
        bp_ref[h] = jnp.where(band, toep, NEG)
        srow = jnp.where(sband, toep[:n_new, :], NEG)
        bsc_ref[h] = srow[:, :WINDOW]
        bsn_ref[h] = srow[:, WINDOW:WINDOW + n_new]


def _bias_tables(rel_bias, n_new, past_len):
    n_tab = 2 * MAX_REL + 1
    pad = (-n_tab) % LANES
    rb = jnp.pad(rel_bias, ((0, 0), (0, pad)))
    return pl.pallas_call(
        functools.partial(_bias_kernel, n_new=n_new, past_len=past_len),
        out_shape=[
            jax.ShapeDtypeStruct((ATT_HEADS, ATT_TQ, ATT_TK), F32),
            jax.ShapeDtypeStruct((ATT_HEADS, n_new, WINDOW), F32),
            jax.ShapeDtypeStruct((ATT_HEADS, n_new, n_new), F32),
        ],
        compiler_params=pltpu.CompilerParams(vmem_limit_bytes=VMEM_LIMIT),
        name="relative_bias_tables",
    )(rb)


def _softmax_pv(s_blocks, v_blocks):
    m = s_blocks[0].max(axis=-1, keepdims=True)
    for s in s_blocks[1:]:
        m = jnp.maximum(m, s.max(axis=-1, keepdims=True))
    den = None
    acc = None
    for s, vb in zip(s_blocks, v_blocks):
        e = jnp.exp(s - m)
        part = e.sum(axis=-1, keepdims=True)
        den = part if den is None else den + part
        pv = _dot(e.astype(BF16), vb)
        acc = pv if acc is None else acc + pv
    return acc * (1.0 / den)


def _attn_kernel(q_ref, k0, k1, k2, v0, v1, v2, z_ref, bias_ref, o_ref):
    tq = ATT_TQ
    t = pl.program_id(1)
    scale = ATT_HEAD_DIM ** -0.5
    lane = lax.broadcasted_iota(jnp.int32, (1, LANES), 1)
    first = lane < ATT_HEAD_DIM
    blk_ok = [t >= 2, t >= 1, True]
    for p in range(ATT_WIDTH // LANES):
        sl = slice(p * LANES, (p + 1) * LANES)
        q2 = q_ref[:, sl]
        kb = [k0[:, sl], k1[:, sl], k2[:, sl]]
        vb = [v0[:, sl], v1[:, sl], v2[:, sl]]
        outs = []
        for a in range(2):
            qa = jnp.where(first if a == 0 else ~first, q2, jnp.zeros_like(q2))
            s_blocks = []
            for j in range(3):
                s = _dot_nt(qa, kb[j]) * scale + bias_ref[2 * p + a, :, j * tq:(j + 1) * tq]
                if blk_ok[j] is not True:
                    s = jnp.where(blk_ok[j], s, NEG)
                s_blocks.append(s)
            outs.append(_softmax_pv(s_blocks, vb))
        o2 = jnp.where(first, outs[0], outs[1])
        o_ref[:, sl] = (o2 * z_ref[:, sl].astype(F32)).astype(o_ref.dtype)


def _prompt_attention(q, k, v, za, bias, b, s):
    tq = ATT_TQ
    nt = s // tq
    cur = lambda i, t: (i * nt + t, 0)
    prev1 = lambda i, t: (i * nt + jnp.maximum(t - 1, 0), 0)
    prev2 = lambda i, t: (i * nt + jnp.maximum(t - 2, 0), 0)
    blk = lambda m: pl.BlockSpec((tq, ATT_WIDTH), m)
    return pl.pallas_call(
        _attn_kernel,
        out_shape=jax.ShapeDtypeStruct((b * s, ATT_WIDTH), BF16),
        grid=(b, nt),
        in_specs=[blk(cur), blk(prev2), blk(prev1), blk(cur), blk(prev2), blk(prev1), blk(cur), blk(cur),
                  _const((ATT_HEADS, tq, ATT_TK), lambda i, t: (0, 0, 0))],
        out_specs=blk(cur),
        compiler_params=_params(2),
        name="prompt_band_attention",
    )(q, k, k, k, v, v, v, za, bias)


def _sample_attn_kernel(q_ref, kc_ref, vc_ref, kn_ref, vn_ref, z_ref, bc_ref, bn_ref, o_ref):
    scale = ATT_HEAD_DIM ** -0.5
    n_new = q_ref.shape[0]
    o_ref[n_new:, :] = jnp.zeros((o_ref.shape[0] - n_new, o_ref.shape[1]), o_ref.dtype)
    lane = lax.broadcasted_iota(jnp.int32, (1, LANES), 1)
    first = lane < ATT_HEAD_DIM
    for p in range(ATT_WIDTH // LANES):
        sl = slice(p * LANES, (p + 1) * LANES)
        q2 = q_ref[:, sl]
        kb = [kc_ref[:, sl].astype(BF16), kn_ref[:, sl]]
        vb = [vc_ref[:, sl].astype(BF16), vn_ref[:, sl]]
        outs = []
        for a in range(2):
            qa = jnp.where(first if a == 0 else ~first, q2, jnp.zeros_like(q2))
            h = 2 * p + a
            s_blocks = [_dot_nt(qa, kb[0]) * scale + bc_ref[h], _dot_nt(qa, kb[1]) * scale + bn_ref[h]]
            outs.append(_softmax_pv(s_blocks, vb))
        o2 = jnp.where(first, outs[0], outs[1])
        o_ref[:n_new, sl] = (o2 * z_ref[:, sl].astype(F32)).astype(o_ref.dtype)


def _sample_attention(q, k, v, za, cache_k, cache_v, bias_c, bias_n, nb, n_new, s_pad):
    step = s_pad // n_new
    new = pl.BlockSpec((n_new, ATT_WIDTH), lambda i: (i * step, 0))
    old = pl.BlockSpec((None, WINDOW, ATT_WIDTH), lambda i: (i, 0, 0))
    return pl.pallas_call(
        _sample_attn_kernel,
        out_shape=jax.ShapeDtypeStruct((nb * s_pad, ATT_WIDTH), BF16),
        grid=(nb,),
        in_specs=[new, old, old, new, new, new,
                  _const(bias_c.shape, lambda i: (0, 0, 0)), _const(bias_n.shape, lambda i: (0, 0, 0))],
        out_specs=pl.BlockSpec((s_pad, ATT_WIDTH), lambda i: (i, 0)),
        compiler_params=_params(1),
        name="sample_band_attention",
    )(q, cache_k, cache_v, k, v, za, bias_c, bias_n)


def _ssd_kernel(*refs, chunk, n_valid, has_init):
    if has_init:
        (xbc_ref, dt_ref, dtT_ref, zs_ref, cw_ref, cb_ref, dtb_ref, dtbT_ref, alog_ref, alogT_ref, dsk_ref, ng_ref,
         conv0_ref, h0_ref, y_ref, hout_ref, xpad, xc, xw, hT) = refs
    else:
        (xbc_ref, dt_ref, dtT_ref, zs_ref, cw_ref, cb_ref, dtb_ref, dtbT_ref, alog_ref, alogT_ref, dsk_ref, ng_ref,
         y_ref, hout_ref, xpad, xc, xw, hT) = refs
    L = chunk
    c = pl.program_id(1)
    n_state = SSD_GROUPS * SSD_STATE

    @pl.when(c == 0)
    def _():
        if has_init:
            xpad[0:HALO, :] = conv0_ref[...]
            for g in range(SSD_GROUPS):
                hT[g] = h0_ref[g * GROUP_W:(g + 1) * GROUP_W, :].T
        else:
            xpad[0:HALO, :] = jnp.zeros((HALO, CONV_CH), F32)
            hT[...] = jnp.zeros(hT.shape, F32)

    xpad[HALO:HALO + L, :] = xbc_ref[...].astype(F32)
    cblk = 4 * LANES
    first_tap = HALO - (CONV_WIDTH - 1)
    for c0 in range(0, CONV_CH, cblk):
        acc = cb_ref[:, c0:c0 + cblk]
        for j in range(CONV_WIDTH):
            acc = acc + xpad[first_tap + j:first_tap + j + L, c0:c0 + cblk] * cw_ref[j:j + 1, c0:c0 + cblk]
        xc[:, c0:c0 + cblk] = acc * jax.nn.sigmoid(acc)
    xpad[0:HALO, :] = xpad[L:L + HALO, :]

    dt = _softplus(dt_ref[...] + dtb_ref[...])
    dtT = _softplus(dtT_ref[...] + dtbT_ref[...])
    if n_valid < L:
        dt = jnp.where(lax.broadcasted_iota(jnp.int32, dt.shape, 0) < n_valid, dt, 0.0)
        dtT = jnp.where(lax.broadcasted_iota(jnp.int32, dtT.shape, 1) < n_valid, dtT, 0.0)
    da = dt * (-jnp.exp(alog_ref[...]))
    daT = dtT * (-jnp.exp(alogT_ref[...]))
    ri = lax.broadcasted_iota(jnp.int32, (L, L), 0)
    ci = lax.broadcasted_iota(jnp.int32, (L, L), 1)
    causal = ri >= ci
    lower = jnp.where(causal, 1.0, 0.0).astype(BF16)
    upper = jnp.where(ri <= ci, 1.0, 0.0).astype(BF16)
    acs = sum(_dot(lower, part) for part in _split3(da))
    acsT = sum(_dot(part, upper) for part in _split3(daT))
    acs_last = acs[L - 1:L, :]
    e_acs = jnp.exp(acs)
    w_end = dt * jnp.exp(acs_last - acs)
    e_last = jnp.exp(acs_last)

    lane = lax.broadcasted_iota(jnp.int32, (1, LANES), 1)
    first = lane < SSD_HEAD_DIM

    def pair_cols(mat, e0, rows):
        a = jnp.broadcast_to(mat[:, e0:e0 + 1], (rows, LANES))
        b = jnp.broadcast_to(mat[:, e0 + 1:e0 + 2], (rows, LANES))
        return jnp.where(first, a, b)

    for g in range(SSD_GROUPS):
        b_g = xc[:, SSD_WIDTH + g * SSD_STATE:SSD_WIDTH + (g + 1) * SSD_STATE]
        c_g = xc[:, SSD_WIDTH + n_state + g * SSD_STATE:SSD_WIDTH + n_state + (g + 1) * SSD_STATE]
        c_gb = c_g.astype(BF16)
        cb = _dot_nt(c_gb, b_g.astype(BF16))
        h_old = hT[g]
        y_state = _dot(c_gb, h_old.astype(BF16))
        y_pairs = []
        decays = []
        for pp in range(HEADS_PER_GROUP // 2):
            e0 = g * HEADS_PER_GROUP + 2 * pp
            col0 = g * GROUP_W + pp * LANES
            xp = xc[:, col0:col0 + LANES]
            y = y_state[:, pp * LANES:(pp + 1) * LANES] * pair_cols(e_acs, e0, L)
            for a in range(2):
                e = e0 + a
                seg = jnp.broadcast_to(acs[:, e:e + 1], (L, L)) - acsT[e:e + 1, :]
                m = cb * jnp.exp(jnp.where(causal, seg, NEG)) * dtT[e:e + 1, :]
                xa = jnp.where(first if a == 0 else ~first, xp, 0.0)
                y = y + _dot(m.astype(BF16), xa.astype(BF16))
            y_pairs.append(y + dsk_ref[:, col0:col0 + LANES] * xp)
            xw[:, pp * LANES:(pp + 1) * LANES] = xp * pair_cols(w_end, e0, L)
            decays.append(pair_cols(e_last, e0, 1))
        hT[g] = h_old * jnp.concatenate(decays, axis=1) + _dot(b_g.T.astype(BF16), xw[...].astype(BF16))
        cols = slice(g * GROUP_W, (g + 1) * GROUP_W)
        yg = jnp.concatenate(y_pairs, axis=1) * zs_ref[:, cols].astype(F32)
        ms = jnp.mean(yg * yg, axis=-1, keepdims=True)
        y_ref[:, cols] = (yg * lax.rsqrt(ms + EPS) * ng_ref[:, cols]).astype(y_ref.dtype)

    @pl.when(c == pl.num_programs(1) - 1)
    def _():
        for g in range(SSD_GROUPS):
            hout_ref[g * GROUP_W:(g + 1) * GROUP_W, :] = hT[g].T


def _ssd(xbc, dt, dtT, zs, wts, b, s, chunk, n_valid, conv0=None, h0=None):
    nc = s // chunk
    has_init = conv0 is not None
    row = lambda i, c: (i * nc + c, 0)
    zero = lambda i, c: (0, 0)
    per_b = lambda i, c: (i, 0, 0)
    names = ("conv_w", "conv_b", "dt_b", "dt_bT", "a_log", "a_logT", "d_skip", "ssd_ng")
    w_list = [wts[n] for n in names]
    in_specs = [
        pl.BlockSpec((chunk, CONV_CH), row),
        pl.BlockSpec((chunk, LANES), row),
        pl.BlockSpec((LANES, chunk), lambda i, c: (0, i * nc + c)),
        pl.BlockSpec((chunk, SSD_WIDTH), row),
    ] + [_const(w.shape, zero) for w in w_list]
    args = [xbc, dt, dtT, zs] + w_list
    if has_init:
        in_specs += [pl.BlockSpec((None, HALO, CONV_CH), per_b),
                     pl.BlockSpec((None, SSD_HEADS * SSD_HEAD_DIM, SSD_STATE), per_b)]
        args += [conv0, h0]
    return pl.pallas_call(
        functools.partial(_ssd_kernel, chunk=chunk, n_valid=n_valid, has_init=has_init),
        out_shape=[jax.ShapeDtypeStruct((b * s, SSD_WIDTH), BF16),
                   jax.ShapeDtypeStruct((b, SSD_HEADS * SSD_HEAD_DIM, SSD_STATE), F32)],
        grid=(b, nc),
        in_specs=in_specs,
        out_specs=[pl.BlockSpec((chunk, SSD_WIDTH), row),
                   pl.BlockSpec((None, SSD_HEADS * SSD_HEAD_DIM, SSD_STATE), per_b)],
        scratch_shapes=[
            pltpu.VMEM((HALO + chunk, CONV_CH), F32),
            pltpu.VMEM((chunk, CONV_CH), F32),
            pltpu.VMEM((chunk, GROUP_W), F32),
            pltpu.VMEM((SSD_GROUPS, SSD_STATE, GROUP_W), F32),
        ],
        compiler_params=_params(2),
        name="conv_ssd_scan",
    )(*args)


def _out_kernel(x_ref, gate_ref, att_ref, y_ref, g_ref, wap, wsp, wout, o_ref):
    att = _dot(att_ref[...], wap[...])
    ssd = _dot(y_ref[...], wsp[...])
    g = g_ref[...].astype(F32)
    merged = g[:, :D_MODEL] * att + g[:, D_MODEL:] * ssd
    o_ref[...] = x_ref[...] + gate_ref[...] * _dot(merged.astype(BF16), wout[...])


def _output(x, gate, att, y, gates, wts, tm):
    b, s, _ = x.shape
    t = b * s
    per_b = s // tm
    row = lambda i: (i, 0)
    zero = lambda i: (0, 0)
    w_list = [wts["w_att_proj"], wts["w_ssd_proj"], wts["w_out"]]
    out = pl.pallas_call(
        _out_kernel,
        out_shape=jax.ShapeDtypeStruct((t, D_MODEL), F32),
        grid=(t // tm,),
        in_specs=[
            pl.BlockSpec((tm, D_MODEL), row),
            pl.BlockSpec((None, 1, D_MODEL), lambda i: (i // per_b, 0, 0)),
            pl.BlockSpec((tm, ATT_WIDTH), row),
            pl.BlockSpec((tm, SSD_WIDTH), row),
            pl.BlockSpec((tm, 2 * D_MODEL), row),
        ] + [_const(w.shape, zero) for w in w_list],
        out_specs=pl.BlockSpec((tm, D_MODEL), row),
        compiler_params=_params(1),
        name="merge_output_projection",
    )(x.reshape(t, D_MODEL), gate, att, y, gates, *w_list)
    return out.reshape(b, s, D_MODEL)


def _layer_weights(w_in, q_norm_g, k_norm_g, w_att_proj, conv_w, conv_b, dt_bias, a_log, d_skip, ssd_norm_g,
                   w_ssd_proj, w_out):
    sizes = (ATT_WIDTH, ATT_WIDTH, ATT_WIDTH, ATT_WIDTH, SSD_WIDTH, CONV_CH, SSD_HEADS, 2 * D_MODEL)
    offs = [0]
    for n in sizes:
        offs.append(offs[-1] + n)
    wq, wk, wv, wza, wzs, wxbc, wdt, wg = (w_in[:, offs[i]:offs[i + 1]].astype(BF16) for i in range(len(sizes)))
    pad_h = LANES - SSD_HEADS
    head_of = jnp.arange(ATT_WIDTH) // ATT_HEAD_DIM
    sel = head_of[:, None] == jnp.arange(LANES)[None, :]
    row_vec = lambda v: jnp.pad(v.astype(F32), (0, pad_h)).reshape(1, LANES)
    return {
        "wq": wq, "wk": wk, "wv": wv, "wza": wza, "wzs": wzs, "wxbc": wxbc, "wg": wg,
        "wdt": jnp.pad(wdt, ((0, 0), (0, pad_h))),
        "wdtT": jnp.pad(wdt.T, ((0, pad_h), (0, 0))),
        "gsum": jnp.where(sel, 1.0 / ATT_HEAD_DIM, 0.0).astype(BF16),
        "gexp": jnp.where(sel.T, 1.0, 0.0).astype(BF16),
        "qg": jnp.tile(q_norm_g.astype(F32), ATT_HEADS).reshape(1, ATT_WIDTH),
        "kg": jnp.tile(k_norm_g.astype(F32), ATT_HEADS).reshape(1, ATT_WIDTH),
        "conv_w": conv_w.astype(F32), "conv_b": conv_b.astype(F32).reshape(1, CONV_CH),
        "dt_b": row_vec(dt_bias), "dt_bT": row_vec(dt_bias).reshape(LANES, 1),
        "a_log": row_vec(a_log), "a_logT": row_vec(a_log).reshape(LANES, 1),
        "d_skip": jnp.repeat(d_skip.astype(F32), SSD_HEAD_DIM).reshape(1, SSD_WIDTH),
        "ssd_ng": ssd_norm_g.astype(F32).reshape(1, SSD_WIDTH),
        "w_att_proj": w_att_proj.astype(BF16), "w_ssd_proj": w_ssd_proj.astype(BF16), "w_out": w_out.astype(BF16),
    }


def _trunk_layer(x, mod, norm_g, wts, attn_fn, chunk, n_valid, conv0=None, h0=None):
    b, s, _ = x.shape
    shift, scale, gate = (mod[:, :, i * D_MODEL:(i + 1) * D_MODEL] for i in range(3))
    tm = min(256, s)
    q, k, v, za, zs, xbc, gates, dt, dtT = _input_projection(x, shift, scale, norm_g, wts, tm)
    att = attn_fn(q, k, v, za)
    y, h_new = _ssd(xbc, dt, dtT, zs, wts, b, s, chunk, n_valid, conv0, h0)
    out = _output(x, gate, att, y, gates, wts, min(512, s))
    return out, k.reshape(b, s, ATT_WIDTH), v.reshape(b, s, ATT_WIDTH), xbc.reshape(b, s, CONV_CH), h_new


def kernel(x_prompt, x_sample, c_prompt, c_sample, cache_k, cache_v, state_conv, state_ssm, norm_g, w_ada, b_ada, w_in, q_norm_g, k_norm_g, rel_bias, w_att_proj, conv_w, conv_b, dt_bias, a_log, d_skip, ssd_norm_g, w_ssd_proj, w_out):
    depth = w_in.shape[0]
    bp, sp, _ = x_prompt.shape
    bs, n_new, _ = x_sample.shape
    rows = cache_k.shape[2]
    past_len = PAST_LEN
    assert rows == WINDOW and n_new <= CHUNK and n_new >= CONV_WIDTH - 1 and sp % SSD_L == 0 and sp % ATT_TQ == 0
    heads = (ATT_HEADS, ATT_HEAD_DIM)
    ssm_shape = (SSD_HEADS, SSD_HEAD_DIM, SSD_STATE)
    n_c = bp + bs
    c_all = jnp.pad(jnp.concatenate([c_prompt, c_sample], axis=0), ((0, (-n_c) % SUBLANES), (0, 0)))
    xp = x_prompt
    xs = jnp.pad(x_sample, ((0, 0), (0, SAMPLE_PAD - n_new), (0, 0)))
    outs = [[] for _ in range(8)]
    for l in range(depth):
        wts = _layer_weights(w_in[l], q_norm_g[l], k_norm_g[l], w_att_proj[l], conv_w[l], conv_b[l], dt_bias[l],
                             a_log[l], d_skip[l], ssd_norm_g[l], w_ssd_proj[l], w_out[l])
        mod = _modulation(c_all, w_ada[l], b_ada[l])[:, None, :]
        bias_p, bias_c, bias_n = _bias_tables(rel_bias[l], n_new, past_len)

        attn_p = functools.partial(_prompt_attention, bias=bias_p, b=bp, s=sp)
        xp, kp, vp, cp, hp = _trunk_layer(xp, mod[:bp], norm_g[l], wts, attn_p, SSD_L, SSD_L)

        attn_s = functools.partial(
            _sample_attention, cache_k=cache_k[l].reshape(bs, rows, ATT_WIDTH),
            cache_v=cache_v[l].reshape(bs, rows, ATT_WIDTH), bias_c=bias_c, bias_n=bias_n,
            nb=bs, n_new=n_new, s_pad=SAMPLE_PAD)
        conv0 = jnp.pad(state_conv[l], ((0, 0), (HALO - (CONV_WIDTH - 1), 0), (0, 0)))
        h0 = state_ssm[l].reshape(bs, SSD_HEADS * SSD_HEAD_DIM, SSD_STATE)
        xs, ks, vs, cs, hs = _trunk_layer(xs, mod[bp:n_c], norm_g[l], wts, attn_s, SAMPLE_PAD, n_new, conv0, h0)

        keep = min(WINDOW, sp)
        outs[0].append(kp[:, sp - keep:].astype(F32).reshape(bp, keep, *heads))
        outs[1].append(vp[:, sp - keep:].astype(F32).reshape(bp, keep, *heads))
        outs[2].append(cp[:, sp - (CONV_WIDTH - 1):].astype(F32))
        outs[3].append(hp.reshape(bp, *ssm_shape))
        outs[4].append(ks[:, :n_new].astype(F32).reshape(bs, n_new, *heads))
        outs[5].append(vs[:, :n_new].astype(F32).reshape(bs, n_new, *heads))
        outs[6].append(cs[:, n_new - (CONV_WIDTH - 1):n_new].astype(F32))
        outs[7].append(hs.reshape(bs, *ssm_shape))
    return (xp, xs[:, :n_new]) + tuple(jnp.stack(o) for o in outs)
```

```python
import functools

import jax
import jax.numpy as jnp
from jax import lax
from jax.experimental import pallas as pl
from jax.experimental.pallas import tpu as pltpu

F32 = jnp.float32
BF16 = jnp.bfloat16

D_MODEL = 1024
CHUNK = 64
BAND_CHUNKS = 8
WINDOW = BAND_CHUNKS * CHUNK
ATT_HEADS = 16
ATT_HEAD_DIM = 64
ATT_WIDTH = ATT_HEADS * ATT_HEAD_DIM
MAX_REL = 256
SSD_WIDTH = 2 * D_MODEL
SSD_HEAD_DIM = 64
SSD_HEADS = SSD_WIDTH // SSD_HEAD_DIM
SSD_GROUPS = 4
SSD_STATE = 128
CONV_WIDTH = 4
CONV_CH = SSD_WIDTH + 2 * SSD_GROUPS * SSD_STATE
PAST_LEN = 4096
EPS = 1e-6
NEG = -1e30

LANES = 128
SUBLANES = 8
VMEM_LIMIT = 56 * 1024 * 1024

ATT_TQ = 256
ATT_TK = ATT_TQ + WINDOW
SSD_L = 256
SAMPLE_PAD = 128
HEADS_PER_GROUP = SSD_HEADS // SSD_GROUPS
GROUP_W = SSD_WIDTH // SSD_GROUPS
HALO = SUBLANES


def _const(shape, index_map):
    return pl.BlockSpec(shape, index_map, pipeline_mode=pl.Buffered(1))


def _params(n_axes):
    return pltpu.CompilerParams(dimension_semantics=("arbitrary",) * n_axes, vmem_limit_bytes=VMEM_LIMIT)


def _dot(a, b):
    return jnp.dot(a, b, preferred_element_type=F32)


def _dot_nt(a, b):
    return lax.dot_general(a, b, (((1,), (1,)), ((), ())), preferred_element_type=F32)


def _split3(x):
    hi = x.astype(BF16)
    r1 = x - hi.astype(F32)
    mid = r1.astype(BF16)
    lo = (r1 - mid.astype(F32)).astype(BF16)
    return hi, mid, lo


def _softplus(x):
    return jnp.maximum(x, 0.0) + jnp.log1p(jnp.exp(-jnp.abs(x)))


def _mod_kernel(c_ref, w_ref, b_ref, o_ref):
    c = c_ref[...]
    a = (c * jax.nn.sigmoid(c)).astype(BF16)
    o_ref[...] = _dot(a, w_ref[...].astype(BF16)) + b_ref[...]


def _modulation(c_all, w_ada, b_ada):
    n = c_all.shape[0]
    return pl.pallas_call(
        _mod_kernel,
        out_shape=jax.ShapeDtypeStruct((n, 3 * D_MODEL), F32),
        grid=(3,),
        in_specs=[
            pl.BlockSpec((n, D_MODEL), lambda j: (0, 0)),
            pl.BlockSpec((D_MODEL, D_MODEL), lambda j: (0, j)),
            pl.BlockSpec((1, D_MODEL), lambda j: (0, j)),
        ],
        out_specs=pl.BlockSpec((n, D_MODEL), lambda j: (0, j)),
        compiler_params=_params(1),
        name="adaln_modulation",
    )(c_all, w_ada, b_ada.reshape(1, -1))


def _pair_cols(mat, e0, rows, first):
    a = jnp.broadcast_to(mat[:, e0:e0 + 1], (rows, LANES))
    b = jnp.broadcast_to(mat[:, e0 + 1:e0 + 2], (rows, LANES))
    return jnp.where(first, a, b)


def _inproj_kernel(x_ref, shift_ref, scale_ref, ng_ref, conv0_ref, wq, wk, wv, wza, wzs, wxbc, wg, wdt, wdtT,
                   gsum, qg, kg, cw_ref, cb_ref, q_o, k_o, v_o, za_o, zs_o, xc_o, g_o, dt_o, dtT_o, tail_o, xpad,
                   *, per_b, n_valid):
    tm = x_ref.shape[0]

    @pl.when(lax.rem(pl.program_id(0), per_b) == 0)
    def _():
        xpad[0:HALO, :] = conv0_ref[...]

    x = x_ref[...]
    ms = jnp.mean(x * x, axis=-1, keepdims=True)
    h = x * lax.rsqrt(ms + EPS) * ng_ref[...]
    h = h * (1.0 + scale_ref[...]) + shift_ref[...]
    hb = h.astype(BF16)
    first = lax.broadcasted_iota(jnp.int32, (1, LANES), 1) < ATT_HEAD_DIM

    def head_norm(y, g_ref):
        ss = _dot((y * y).astype(BF16), gsum[...])
        r = lax.rsqrt(ss + EPS)
        parts = [y[:, p * LANES:(p + 1) * LANES] * _pair_cols(r, 2 * p, tm, first)
                 for p in range(ATT_WIDTH // LANES)]
        return jnp.concatenate(parts, axis=1) * g_ref[...]

    def conv_block(c0, cblk):
        xpad[HALO:HALO + tm, c0:c0 + cblk] = _dot(hb, wxbc[:, c0:c0 + cblk])
        first_tap = HALO - (CONV_WIDTH - 1)
        acc = cb_ref[:, c0:c0 + cblk]
        for j in range(CONV_WIDTH):
            acc = acc + xpad[first_tap + j:first_tap + j + tm, c0:c0 + cblk] * cw_ref[j:j + 1, c0:c0 + cblk]
        xc_o[:, c0:c0 + cblk] = (acc * jax.nn.sigmoid(acc)).astype(xc_o.dtype)

    cblk = CONV_CH // 6
    conv_block(0 * cblk, cblk)
    q_o[...] = head_norm(_dot(hb, wq[...]), qg).astype(q_o.dtype)
    conv_block(1 * cblk, cblk)
    k_o[...] = head_norm(_dot(hb, wk[...]), kg).astype(k_o.dtype)
    conv_block(2 * cblk, cblk)
    v_o[...] = _dot(hb, wv[...]).astype(v_o.dtype)
    conv_block(3 * cblk, cblk)
    za = _dot(hb, wza[...])
    za_o[...] = (za * jax.nn.sigmoid(za)).astype(za_o.dtype)
    conv_block(4 * cblk, cblk)
    zs = _dot(hb, wzs[...])
    zs_o[...] = (zs * jax.nn.sigmoid(zs)).astype(zs_o.dtype)
    conv_block(5 * cblk, cblk)
    g_o[...] = jax.nn.sigmoid(_dot(hb, wg[...])).astype(g_o.dtype)
    dt_o[...] = _dot(hb, wdt[...])
    dtT_o[...] = _dot_nt(wdtT[...], hb)
    tail_o[...] = xpad[n_valid:n_valid + HALO, :]
    xpad[0:HALO, :] = xpad[tm:tm + HALO, :]


def _input_projection(x, shift, scale, norm_g, conv0, wts, tm, n_valid):
    b, s, _ = x.shape
    t = b * s
    per_b = s // tm
    x2 = x.reshape(t, D_MODEL)
    row = lambda i: (i, 0)
    bat = lambda i: (i // per_b, 0, 0)
    zero = lambda i: (0, 0)
    w_names = ("wq", "wk", "wv", "wza", "wzs", "wxbc", "wg", "wdt", "wdtT", "gsum", "qg", "kg", "conv_w", "conv_b")
    w_list = [wts[n] for n in w_names]
    widths = (ATT_WIDTH, ATT_WIDTH, ATT_WIDTH, ATT_WIDTH, SSD_WIDTH, CONV_CH, 2 * D_MODEL)
    out_shape = [jax.ShapeDtypeStruct((t, w), BF16) for w in widths]
    out_shape += [jax.ShapeDtypeStruct((t, LANES), F32), jax.ShapeDtypeStruct((LANES, t), F32),
                  jax.ShapeDtypeStruct((b, HALO, CONV_CH), F32)]
    out_specs = [pl.BlockSpec((tm, w), row) for w in widths]
    out_specs += [pl.BlockSpec((tm, LANES), row), pl.BlockSpec((LANES, tm), lambda i: (0, i)),
                  pl.BlockSpec((None, HALO, CONV_CH), bat)]
    return pl.pallas_call(
        functools.partial(_inproj_kernel, per_b=per_b, n_valid=n_valid),
        out_shape=out_shape,
        grid=(t // tm,),
        in_specs=[
            pl.BlockSpec((tm, D_MODEL), row),
            pl.BlockSpec((None, 1, D_MODEL), bat),
            pl.BlockSpec((None, 1, D_MODEL), bat),
            _const((1, D_MODEL), zero),
            pl.BlockSpec((None, HALO, CONV_CH), bat),
        ] + [_const(w.shape, zero) for w in w_list],
        out_specs=out_specs,
        scratch_shapes=[pltpu.VMEM((HALO + tm, CONV_CH), F32)],
        compiler_params=_params(1),
        name="input_projection",
    )(x2, shift, scale, norm_g.reshape(1, -1), conv0, *w_list)


def _bias_kernel(rb_ref, bp_ref, bsc_ref, bsn_ref, *, n_new, past_len):
    tq, tk = ATT_TQ, ATT_TK
    w = tq + tk
    n_tab = rb_ref.shape[1]
    v = lax.broadcasted_iota(jnp.int32, (n_tab, w), 1)
    d = lax.broadcasted_iota(jnp.int32, (n_tab, w), 0)
    idx = jnp.clip(tq - v + WINDOW, -MAX_REL, MAX_REL) + MAX_REL
    onehot = jnp.where(idx == d, 1.0, 0.0).astype(BF16)
    hi, mid, lo = _split3(rb_ref[...])
    base = _dot(hi, onehot) + _dot(mid, onehot) + _dot(lo, onehot)

    chunk_of = lambda pos: jnp.right_shift(pos, CHUNK.bit_length() - 1)
    qi = chunk_of(lax.broadcasted_iota(jnp.int32, (tq, tk), 0))
    kj = chunk_of(lax.broadcasted_iota(jnp.int32, (tq, tk), 1))
    band = (kj >= qi) & (kj <= qi + BAND_CHUNKS)
    sq = chunk_of(past_len + lax.broadcasted_iota(jnp.int32, (n_new, tk), 0))
    sk = chunk_of(past_len - WINDOW + lax.broadcasted_iota(jnp.int32, (n_new, tk), 1))
    sband = (sk <= sq) & (sk >= sq - BAND_CHUNKS)
    for h in range(ATT_HEADS):
        rows = jnp.broadcast_to(base[h:h + 1, :], (tq, w))
        toep = pltpu.roll(rows, 0, 1, stride=1, stride_axis=0)[:, tq:]
        bp_ref[h] = jnp.where(band, toep, NEG)
        srow = jnp.where(sband, toep[:n_new, :], NEG)
        bsc_ref[h] = srow[:, :WINDOW]
        bsn_ref[h] = srow[:, WINDOW:WINDOW + n_new]


def _bias_tables(rel_bias, n_new, past_len):
    n_tab = 2 * MAX_REL + 1
    pad = (-n_tab) % LANES
    rb = jnp.pad(rel_bias, ((0, 0), (0, pad)))
    return pl.pallas_call(
        functools.partial(_bias_kernel, n_new=n_new, past_len=past_len),
        out_shape=[
            jax.ShapeDtypeStruct((ATT_HEADS, ATT_TQ, ATT_TK), F32),
            jax.ShapeDtypeStruct((ATT_HEADS, n_new, WINDOW), F32),
            jax.ShapeDtypeStruct((ATT_HEADS, n_new, n_new), F32),
        ],
        compiler_params=pltpu.CompilerParams(vmem_limit_bytes=VMEM_LIMIT),
        name="relative_bias_tables",
    )(rb)


def _softmax_pv(s_blocks, v_blocks):
    m = s_blocks[0].max(axis=-1, keepdims=True)
    for s in s_blocks[1:]:
        m = jnp.maximum(m, s.max(axis=-1, keepdims=True))
    den = None
    acc = None
    for s, vb in zip(s_blocks, v_blocks):
        e = jnp.exp(s - m)
        part = e.sum(axis=-1, keepdims=True)
        den = part if den is None else den + part
        pv = _dot(e.astype(BF16), vb)
        acc = pv if acc is None else acc + pv
    return acc * (1.0 / den)


def _attn_kernel(q_ref, k0, k1, k2, v0, v1, v2, z_ref, bias_ref, o_ref):
    tq = ATT_TQ
    t = pl.program_id(1)
    scale = ATT_HEAD_DIM ** -0.5
    lane = lax.broadcasted_iota(jnp.int32, (1, LANES), 1)
    first = lane < ATT_HEAD_DIM
    blk_ok = [t >= 2, t >= 1, True]
    for p in range(ATT_WIDTH // LANES):
        sl = slice(p * LANES, (p + 1) * LANES)
        q2 = q_ref[:, sl]
        kb = [k0[:, sl], k1[:, sl], k2[:, sl]]
        vb = [v0[:, sl], v1[:, sl], v2[:, sl]]
        outs = []
        for a in range(2):
            qa = jnp.where(first if a == 0 else ~first, q2, jnp.zeros_like(q2))
            s_blocks = []
            for j in range(3):
                s = _dot_nt(qa, kb[j]) * scale + bias_ref[2 * p + a, :, j * tq:(j + 1) * tq]
                if blk_ok[j] is not True:
                    s = jnp.where(blk_ok[j], s, NEG)
                s_blocks.append(s)
            outs.append(_softmax_pv(s_blocks, vb))
        o2 = jnp.where(first, outs[0], outs[1])
        o_ref[:, sl] = (o2 * z_ref[:, sl].astype(F32)).astype(o_ref.dtype)


def _prompt_attention(q, k, v, za, bias, b, s):
    tq = ATT_TQ
    nt = s // tq
    cur = lambda i, t: (i * nt + t, 0)
    prev1 = lambda i, t: (i * nt + jnp.maximum(t - 1, 0), 0)
    prev2 = lambda i, t: (i * nt + jnp.maximum(t - 2, 0), 0)
    blk = lambda m: pl.BlockSpec((tq, ATT_WIDTH), m)
    return pl.pallas_call(
        _attn_kernel,
        out_shape=jax.ShapeDtypeStruct((b * s, ATT_WIDTH), BF16),
        grid=(b, nt),
        in_specs=[blk(cur), blk(prev2), blk(prev1), blk(cur), blk(prev2), blk(prev1), blk(cur), blk(cur),
                  _const((ATT_HEADS, tq, ATT_TK), lambda i, t: (0, 0, 0))],
        out_specs=blk(cur),
        compiler_params=_params(2),
        name="prompt_band_attention",
    )(q, k, k, k, v, v, v, za, bias)


def _sample_attn_kernel(q_ref, kc_ref, vc_ref, kn_ref, vn_ref, z_ref, bc_ref, bn_ref, o_ref):
    scale = ATT_HEAD_DIM ** -0.5
    n_new = q_ref.shape[0]
    o_ref[n_new:, :] = jnp.zeros((o_ref.shape[0] - n_new, o_ref.shape[1]), o_ref.dtype)
    lane = lax.broadcasted_iota(jnp.int32, (1, LANES), 1)
    first = lane < ATT_HEAD_DIM
    for p in range(ATT_WIDTH // LANES):
        sl = slice(p * LANES, (p + 1) * LANES)
        q2 = q_ref[:, sl]
        kb = [kc_ref[:, sl].astype(BF16), kn_ref[:, sl]]
        vb = [vc_ref[:, sl].astype(BF16), vn_ref[:, sl]]
        outs = []
        for a in range(2):
            qa = jnp.where(first if a == 0 else ~first, q2, jnp.zeros_like(q2))
            h = 2 * p + a
            s_blocks = [_dot_nt(qa, kb[0]) * scale + bc_ref[h], _dot_nt(qa, kb[1]) * scale + bn_ref[h]]
            outs.append(_softmax_pv(s_blocks, vb))
        o2 = jnp.where(first, outs[0], outs[1])
        o_ref[:n_new, sl] = (o2 * z_ref[:, sl].astype(F32)).astype(o_ref.dtype)


def _sample_attention(q, k, v, za, cache_k, cache_v, bias_c, bias_n, nb, n_new, s_pad):
    step = s_pad // n_new
    new = pl.BlockSpec((n_new, ATT_WIDTH), lambda i: (i * step, 0))
    old = pl.BlockSpec((None, WINDOW, ATT_WIDTH), lambda i: (i, 0, 0))
    return pl.pallas_call(
        _sample_attn_kernel,
        out_shape=jax.ShapeDtypeStruct((nb * s_pad, ATT_WIDTH), BF16),
        grid=(nb,),
        in_specs=[new, old, old, new, new, new,
                  _const(bias_c.shape, lambda i: (0, 0, 0)), _const(bias_n.shape, lambda i: (0, 0, 0))],
        out_specs=pl.BlockSpec((s_pad, ATT_WIDTH), lambda i: (i, 0)),
        compiler_params=_params(1),
        name="sample_band_attention",
    )(q, cache_k, cache_v, k, v, za, bias_c, bias_n)


def _ssd_kernel(*refs, chunk, n_valid, has_init):
    if has_init:
        (xc_ref, dt_ref, dtT_ref, dtb_ref, dtbT_ref, alog_ref, alogT_ref, dsk_ref, h0_ref,
         y_ref, hout_ref, xw, hT) = refs
    else:
        (xc_ref, dt_ref, dtT_ref, dtb_ref, dtbT_ref, alog_ref, alogT_ref, dsk_ref,
         y_ref, hout_ref, xw, hT) = refs
    L = chunk
    c = pl.program_id(1)
    n_state = SSD_GROUPS * SSD_STATE

    @pl.when(c == 0)
    def _():
        if has_init:
            for g in range(SSD_GROUPS):
                hT[g] = h0_ref[g * GROUP_W:(g + 1) * GROUP_W, :].T
        else:
            hT[...] = jnp.zeros(hT.shape, F32)

    dt = _softplus(dt_ref[...] + dtb_ref[...])
    dtT = _softplus(dtT_ref[...] + dtbT_ref[...])
    if n_valid < L:
        dt = jnp.where(lax.broadcasted_iota(jnp.int32, dt.shape, 0) < n_valid, dt, 0.0)
        dtT = jnp.where(lax.broadcasted_iota(jnp.int32, dtT.shape, 1) < n_valid, dtT, 0.0)
    da = dt * (-jnp.exp(alog_ref[...]))
    daT = dtT * (-jnp.exp(alogT_ref[...]))
    ri = lax.broadcasted_iota(jnp.int32, (L, L), 0)
    ci = lax.broadcasted_iota(jnp.int32, (L, L), 1)
    causal = ri >= ci
    lower = jnp.where(causal, 1.0, 0.0).astype(BF16)
    upper = jnp.where(ri <= ci, 1.0, 0.0).astype(BF16)
    acs = sum(_dot(lower, part) for part in _split3(da))
    acsT = sum(_dot(part, upper) for part in _split3(daT))
    acs_last = acs[L - 1:L, :]
    e_acs = jnp.exp(acs)
    w_end = dt * jnp.exp(acs_last - acs)
    e_last = jnp.exp(acs_last)
    row_term = acsT - jnp.log(dtT)

    first = lax.broadcasted_iota(jnp.int32, (1, LANES), 1) < SSD_HEAD_DIM

    for g in range(SSD_GROUPS):
        b_g = xc_ref[:, SSD_WIDTH + g * SSD_STATE:SSD_WIDTH + (g + 1) * SSD_STATE]
        c_g = xc_ref[:, SSD_WIDTH + n_state + g * SSD_STATE:SSD_WIDTH + n_state + (g + 1) * SSD_STATE]
        cb = _dot_nt(c_g, b_g)
        h_old = hT[g]
        y_state = _dot(c_g, h_old.astype(BF16))
        decays = []
        for pp in range(HEADS_PER_GROUP // 2):
            e0 = g * HEADS_PER_GROUP + 2 * pp
            col0 = g * GROUP_W + pp * LANES
            xp_b = xc_ref[:, col0:col0 + LANES]
            xp = xp_b.astype(F32)
            y = y_state[:, pp * LANES:(pp + 1) * LANES] * _pair_cols(e_acs, e0, L, first)
            for a in range(2):
                e = e0 + a
                seg = jnp.broadcast_to(acs[:, e:e + 1], (L, L)) - row_term[e:e + 1, :]
                m = cb * jnp.exp(jnp.where(causal, seg, NEG))
                xa = jnp.where(first if a == 0 else ~first, xp_b, jnp.zeros_like(xp_b))
                y = y + _dot(m.astype(BF16), xa)
            y_ref[:, col0:col0 + LANES] = (y + dsk_ref[:, col0:col0 + LANES] * xp).astype(y_ref.dtype)
            xw[:, pp * LANES:(pp + 1) * LANES] = (xp * _pair_cols(w_end, e0, L, first)).astype(xw.dtype)
            decays.append(_pair_cols(e_last, e0, 1, first))
        b_gt = b_g.astype(F32).T.astype(BF16)
        hT[g] = h_old * jnp.concatenate(decays, axis=1) + _dot(b_gt, xw[...])

    @pl.when(c == pl.num_programs(1) - 1)
    def _():
        for g in range(SSD_GROUPS):
            hout_ref[g * GROUP_W:(g + 1) * GROUP_W, :] = hT[g].T


def _ssd(xc, dt, dtT, wts, b, s, chunk, n_valid, h0=None):
    nc = s // chunk
    has_init = h0 is not None
    row = lambda i, c: (i * nc + c, 0)
    zero = lambda i, c: (0, 0)
    per_b = lambda i, c: (i, 0, 0)
    names = ("dt_b", "dt_bT", "a_log", "a_logT", "d_skip")
    w_list = [wts[n] for n in names]
    in_specs = [
        pl.BlockSpec((chunk, CONV_CH), row),
        pl.BlockSpec((chunk, LANES), row),
        pl.BlockSpec((LANES, chunk), lambda i, c: (0, i * nc + c)),
    ] + [_const(w.shape, zero) for w in w_list]
    args = [xc, dt, dtT] + w_list
    if has_init:
        in_specs += [pl.BlockSpec((None, SSD_HEADS * SSD_HEAD_DIM, SSD_STATE), per_b)]
        args += [h0]
    return pl.pallas_call(
        functools.partial(_ssd_kernel, chunk=chunk, n_valid=n_valid, has_init=has_init),
        out_shape=[jax.ShapeDtypeStruct((b * s, SSD_WIDTH), BF16),
                   jax.ShapeDtypeStruct((b, SSD_HEADS * SSD_HEAD_DIM, SSD_STATE), F32)],
        grid=(b, nc),
        in_specs=in_specs,
        out_specs=[pl.BlockSpec((chunk, SSD_WIDTH), row),
                   pl.BlockSpec((None, SSD_HEADS * SSD_HEAD_DIM, SSD_STATE), per_b)],
        scratch_shapes=[
            pltpu.VMEM((chunk, GROUP_W), BF16),
            pltpu.VMEM((SSD_GROUPS, SSD_STATE, GROUP_W), F32),
        ],
        compiler_params=_params(2),
        name="ssd_scan",
    )(*args)


def _out_kernel(x_ref, gate_ref, att_ref, y_ref, zs_ref, g_ref, ng_ref, wap, wsp, wout, o_ref):
    att = _dot(att_ref[...], wap[...])
    ssd = None
    for g in range(SSD_GROUPS):
        cols = slice(g * GROUP_W, (g + 1) * GROUP_W)
        yg = y_ref[:, cols].astype(F32) * zs_ref[:, cols].astype(F32)
        ms = jnp.mean(yg * yg, axis=-1, keepdims=True)
        yn = (yg * lax.rsqrt(ms + EPS) * ng_ref[:, cols]).astype(BF16)
        part = _dot(yn, wsp[cols, :])
        ssd = part if ssd is None else ssd + part
    gates = g_ref[...].astype(F32)
    merged = gates[:, :D_MODEL] * att + gates[:, D_MODEL:] * ssd
    o_ref[...] = x_ref[...] + gate_ref[...] * _dot(merged.astype(BF16), wout[...])


def _output(x, gate, att, y, zs, gates, wts, tm):
    b, s, _ = x.shape
    t = b * s
    per_b = s // tm
    row = lambda i: (i, 0)
    zero = lambda i: (0, 0)
    w_list = [wts["ssd_ng"], wts["w_att_proj"], wts["w_ssd_proj"], wts["w_out"]]
    out = pl.pallas_call(
        _out_kernel,
        out_shape=jax.ShapeDtypeStruct((t, D_MODEL), F32),
        grid=(t // tm,),
        in_specs=[
            pl.BlockSpec((tm, D_MODEL), row),
            pl.BlockSpec((None, 1, D_MODEL), lambda i: (i // per_b, 0, 0)),
            pl.BlockSpec((tm, ATT_WIDTH), row),
            pl.BlockSpec((tm, SSD_WIDTH), row),
            pl.BlockSpec((tm, SSD_WIDTH), row),
            pl.BlockSpec((tm, 2 * D_MODEL), row),
        ] + [_const(w.shape, zero) for w in w_list],
        out_specs=pl.BlockSpec((tm, D_MODEL), row),
        compiler_params=_params(1),
        name="merge_output_projection",
    )(x.reshape(t, D_MODEL), gate, att, y, zs, gates, *w_list)
    return out.reshape(b, s, D_MODEL)


def _layer_weights(w_in, q_norm_g, k_norm_g, w_att_proj, conv_w, conv_b, dt_bias, a_log, d_skip, ssd_norm_g,
                   w_ssd_proj, w_out):
    sizes = (ATT_WIDTH, ATT_WIDTH, ATT_WIDTH, ATT_WIDTH, SSD_WIDTH, CONV_CH, SSD_HEADS, 2 * D_MODEL)
    offs = [0]
    for n in sizes:
        offs.append(offs[-1] + n)
    wq, wk, wv, wza, wzs, wxbc, wdt, wg = (w_in[:, offs[i]:offs[i + 1]].astype(BF16) for i in range(len(sizes)))
    pad_h = LANES - SSD_HEADS
    head_of = jnp.arange(ATT_WIDTH) // ATT_HEAD_DIM
    sel = head_of[:, None] == jnp.arange(LANES)[None, :]
    row_vec = lambda v: jnp.pad(v.astype(F32), (0, pad_h)).reshape(1, LANES)
    return {
        "wq": wq, "wk": wk, "wv": wv, "wza": wza, "wzs": wzs, "wxbc": wxbc, "wg": wg,
        "wdt": jnp.pad(wdt, ((0, 0), (0, pad_h))),
        "wdtT": jnp.pad(wdt.T, ((0, pad_h), (0, 0))),
        "gsum": jnp.where(sel, 1.0 / ATT_HEAD_DIM, 0.0).astype(BF16),
        "qg": jnp.tile(q_norm_g.astype(F32), ATT_HEADS).reshape(1, ATT_WIDTH),
        "kg": jnp.tile(k_norm_g.astype(F32), ATT_HEADS).reshape(1, ATT_WIDTH),
        "conv_w": conv_w.astype(F32), "conv_b": conv_b.astype(F32).reshape(1, CONV_CH),
        "dt_b": row_vec(dt_bias), "dt_bT": row_vec(dt_bias).reshape(LANES, 1),
        "a_log": row_vec(a_log), "a_logT": row_vec(a_log).reshape(LANES, 1),
        "d_skip": jnp.repeat(d_skip.astype(F32), SSD_HEAD_DIM).reshape(1, SSD_WIDTH),
        "ssd_ng": ssd_norm_g.astype(F32).reshape(1, SSD_WIDTH),
        "w_att_proj": w_att_proj.astype(BF16), "w_ssd_proj": w_ssd_proj.astype(BF16), "w_out": w_out.astype(BF16),
    }


def _trunk_layer(x, mod, norm_g, wts, attn_fn, conv0, chunk, n_valid, h0=None):
    b, s, _ = x.shape
    shift, scale, gate = (mod[:, :, i * D_MODEL:(i + 1) * D_MODEL] for i in range(3))
    tm = min(256, s)
    q, k, v, za, zs, xc, gates, dt, dtT, tail = _input_projection(x, shift, scale, norm_g, conv0, wts, tm,
                                                                 min(n_valid, tm))
    att = attn_fn(q, k, v, za)
    y, h_new = _ssd(xc, dt, dtT, wts, b, s, chunk, n_valid, h0)
    out = _output(x, gate, att, y, zs, gates, wts, min(512, s))
    return out, k.reshape(b, s, ATT_WIDTH), v.reshape(b, s, ATT_WIDTH), tail[:, HALO - (CONV_WIDTH - 1):], h_new


def kernel(x_prompt, x_sample, c_prompt, c_sample, cache_k, cache_v, state_conv, state_ssm, norm_g, w_ada, b_ada, w_in, q_norm_g, k_norm_g, rel_bias, w_att_proj, conv_w, conv_b, dt_bias, a_log, d_skip, ssd_norm_g, w_ssd_proj, w_out):
    depth = w_in.shape[0]
    bp, sp, _ = x_prompt.shape
    bs, n_new, _ = x_sample.shape
    rows = cache_k.shape[2]
    assert rows == WINDOW and n_new <= CHUNK and n_new % SUBLANES == 0 and sp % SSD_L == 0 and sp % ATT_TQ == 0
    heads = (ATT_HEADS, ATT_HEAD_DIM)
    ssm_shape = (SSD_HEADS, SSD_HEAD_DIM, SSD_STATE)
    n_c = bp + bs
    c_all = jnp.pad(jnp.concatenate([c_prompt, c_sample], axis=0), ((0, (-n_c) % SUBLANES), (0, 0)))
    xp = x_prompt
    xs = jnp.pad(x_sample, ((0, 0), (0, SAMPLE_PAD - n_new), (0, 0)))
    outs = [[] for _ in range(8)]
    for l in range(depth):
        wts = _layer_weights(w_in[l], q_norm_g[l], k_norm_g[l], w_att_proj[l], conv_w[l], conv_b[l], dt_bias[l],
                             a_log[l], d_skip[l], ssd_norm_g[l], w_ssd_proj[l], w_out[l])
        mod = _modulation(c_all, w_ada[l], b_ada[l])[:, None, :]
        bias_p, bias_c, bias_n = _bias_tables(rel_bias[l], n_new, PAST_LEN)

        attn_p = functools.partial(_prompt_attention, bias=bias_p, b=bp, s=sp)
        conv0_p = jnp.zeros((bp, HALO, CONV_CH), F32)
        xp, kp, vp, cp, hp = _trunk_layer(xp, mod[:bp], norm_g[l], wts, attn_p, conv0_p, SSD_L, SSD_L)

        attn_s = functools.partial(
            _sample_attention, cache_k=cache_k[l].reshape(bs, rows, ATT_WIDTH),
            cache_v=cache_v[l].reshape(bs, rows, ATT_WIDTH), bias_c=bias_c, bias_n=bias_n,
            nb=bs, n_new=n_new, s_pad=SAMPLE_PAD)
        conv0_s = jnp.pad(state_conv[l].astype(F32), ((0, 0), (HALO - (CONV_WIDTH - 1), 0), (0, 0)))
        h0 = state_ssm[l].reshape(bs, SSD_HEADS * SSD_HEAD_DIM, SSD_STATE)
        xs, ks, vs, cs, hs = _trunk_layer(xs, mod[bp:n_c], norm_g[l], wts, attn_s, conv0_s, SAMPLE_PAD, n_new, h0)

        keep = min(WINDOW, sp)
        outs[0].append(kp[:, sp - keep:].astype(F32).reshape(bp, keep, *heads))
        outs[1].append(vp[:, sp - keep:].astype(F32).reshape(bp, keep, *heads))
        outs[2].append(cp)
        outs[3].append(hp.reshape(bp, *ssm_shape))
        outs[4].append(ks[:, :n_new].astype(F32).reshape(bs, n_new, *heads))
        outs[5].append(vs[:, :n_new].astype(F32).reshape(bs, n_new, *heads))
        outs[6].append(cs)
        outs[7].append(hs.reshape(bs, *ssm_shape))
    return (xp, xs[:, :n_new]) + tuple(jnp.stack(o) for o in outs)
```

```python
import functools

import jax
import jax.numpy as jnp
from jax import lax
from jax.experimental import pallas as pl
from jax.experimental.pallas import tpu as pltpu

F32 = jnp.float32
BF16 = jnp.bfloat16

D_MODEL = 1024
CHUNK = 64
BAND_CHUNKS = 8
WINDOW = BAND_CHUNKS * CHUNK
ATT_HEADS = 16
ATT_HEAD_DIM = 64
ATT_WIDTH = ATT_HEADS * ATT_HEAD_DIM
MAX_REL = 256
SSD_WIDTH = 2 * D_MODEL
SSD_HEAD_DIM = 64
SSD_HEADS = SSD_WIDTH // SSD_HEAD_DIM
SSD_GROUPS = 4
SSD_STATE = 128
CONV_WIDTH = 4
CONV_CH = SSD_WIDTH + 2 * SSD_GROUPS * SSD_STATE
PAST_LEN = 4096
EPS = 1e-6
NEG = -1e30
LOG2E = 1.4426950408889634

LANES = 128
SUBLANES = 8
VMEM_LIMIT = 56 * 1024 * 1024

IN_TM = 256
OUT_TM = 512
ATT_TQ = 256
ATT_TK = ATT_TQ + WINDOW
SSD_L = 256
SAMPLE_PAD = 128
HEADS_PER_GROUP = SSD_HEADS // SSD_GROUPS
GROUP_W = SSD_WIDTH // SSD_GROUPS
HALO = SUBLANES


def _const(shape, index_map):
    return pl.BlockSpec(shape, index_map, pipeline_mode=pl.Buffered(1))


def _params(n_axes):
    return pltpu.CompilerParams(dimension_semantics=("arbitrary",) * n_axes, vmem_limit_bytes=VMEM_LIMIT)


def _dot(a, b):
    return jnp.dot(a, b, preferred_element_type=F32)


def _dot_nt(a, b):
    return lax.dot_general(a, b, (((1,), (1,)), ((), ())), preferred_element_type=F32)


def _split3(x):
    hi = x.astype(BF16)
    r1 = x - hi.astype(F32)
    mid = r1.astype(BF16)
    lo = (r1 - mid.astype(F32)).astype(BF16)
    return hi, mid, lo


def _sigmoid(x):
    return 0.5 * jnp.tanh(0.5 * x) + 0.5


def _silu(x):
    h = 0.5 * x
    return h * jnp.tanh(h) + h


def _softplus(x):
    return jnp.maximum(x, 0.0) + jnp.log1p(jnp.exp(-jnp.abs(x)))


def _mod_kernel(c_ref, w_ref, b_ref, o_ref):
    c = c_ref[...]
    a = (c * jax.nn.sigmoid(c)).astype(BF16)
    o_ref[...] = _dot(a, w_ref[...].astype(BF16)) + b_ref[...]


def _modulation(c_all, w_ada, b_ada):
    n = c_all.shape[0]
    return pl.pallas_call(
        _mod_kernel,
        out_shape=jax.ShapeDtypeStruct((n, 3 * D_MODEL), F32),
        grid=(3,),
        in_specs=[
            pl.BlockSpec((n, D_MODEL), lambda j: (0, 0)),
            pl.BlockSpec((D_MODEL, D_MODEL), lambda j: (0, j)),
            pl.BlockSpec((1, D_MODEL), lambda j: (0, j)),
        ],
        out_specs=pl.BlockSpec((n, D_MODEL), lambda j: (0, j)),
        compiler_params=_params(1),
        name="adaln_modulation",
    )(c_all, w_ada, b_ada.reshape(1, -1))


def _pair_cols(mat, e0, rows, first):
    a = jnp.broadcast_to(mat[:, e0:e0 + 1], (rows, LANES))
    b = jnp.broadcast_to(mat[:, e0 + 1:e0 + 2], (rows, LANES))
    return jnp.where(first, a, b)


def _inproj_kernel(x_ref, shift_ref, scale_ref, ng_ref, conv0_ref, wq, wk, wv, wza, wzs, wxbc, wg, wdt, wdtT,
                   gsum, qg, kg, cw_ref, cb_ref, q_o, k_o, v_o, za_o, zs_o, xc_o, g_o, dt_o, dtT_o, tail_o, xpad,
                   *, per_b, n_valid):
    tm = x_ref.shape[0]

    @pl.when(lax.rem(pl.program_id(0), per_b) == 0)
    def _():
        xpad[0:HALO, :] = conv0_ref[...]

    x = x_ref[...]
    ms = jnp.mean(x * x, axis=-1, keepdims=True)
    h = x * lax.rsqrt(ms + EPS) * ng_ref[...]
    h = h * (1.0 + scale_ref[...]) + shift_ref[...]
    hb = h.astype(BF16)
    first = lax.broadcasted_iota(jnp.int32, (1, LANES), 1) < ATT_HEAD_DIM

    def head_norm(y, g_ref):
        ss = _dot((y * y).astype(BF16), gsum[...])
        r = lax.rsqrt(ss + EPS)
        parts = [y[:, p * LANES:(p + 1) * LANES] * _pair_cols(r, 2 * p, tm, first)
                 for p in range(ATT_WIDTH // LANES)]
        return jnp.concatenate(parts, axis=1) * g_ref[...]

    def conv_block(c0, cblk):
        xpad[HALO:HALO + tm, c0:c0 + cblk] = _dot(hb, wxbc[:, c0:c0 + cblk])
        first_tap = HALO - (CONV_WIDTH - 1)
        acc = cb_ref[:, c0:c0 + cblk]
        for j in range(CONV_WIDTH):
            acc = acc + xpad[first_tap + j:first_tap + j + tm, c0:c0 + cblk] * cw_ref[j:j + 1, c0:c0 + cblk]
        xc_o[:, c0:c0 + cblk] = _silu(acc).astype(xc_o.dtype)

    cblk = CONV_CH // 6
    conv_block(0 * cblk, cblk)
    q_o[...] = head_norm(_dot(hb, wq[...]), qg).astype(q_o.dtype)
    conv_block(1 * cblk, cblk)
    k_o[...] = head_norm(_dot(hb, wk[...]), kg).astype(k_o.dtype)
    conv_block(2 * cblk, cblk)
    v_o[...] = _dot(hb, wv[...]).astype(v_o.dtype)
    conv_block(3 * cblk, cblk)
    za_o[...] = _silu(_dot(hb, wza[...])).astype(za_o.dtype)
    conv_block(4 * cblk, cblk)
    zs_o[...] = _silu(_dot(hb, wzs[...])).astype(zs_o.dtype)
    conv_block(5 * cblk, cblk)
    g_o[...] = _sigmoid(_dot(hb, wg[...])).astype(g_o.dtype)
    dt_o[...] = _dot(hb, wdt[...])
    dtT_o[...] = _dot_nt(wdtT[...], hb)
    tail_o[...] = xpad[n_valid:n_valid + HALO, :]
    xpad[0:HALO, :] = xpad[tm:tm + HALO, :]


def _input_projection(x, shift, scale, norm_g, conv0, wts, tm, n_valid):
    b, s, _ = x.shape
    t = b * s
    per_b = s // tm
    x2 = x.reshape(t, D_MODEL)
    row = lambda i: (i, 0)
    bat = lambda i: (i // per_b, 0, 0)
    zero = lambda i: (0, 0)
    w_names = ("wq", "wk", "wv", "wza", "wzs", "wxbc", "wg", "wdt", "wdtT", "gsum", "qg", "kg", "conv_w", "conv_b")
    w_list = [wts[n] for n in w_names]
    widths = (ATT_WIDTH, ATT_WIDTH, ATT_WIDTH, ATT_WIDTH, SSD_WIDTH, CONV_CH, 2 * D_MODEL)
    out_shape = [jax.ShapeDtypeStruct((t, w), BF16) for w in widths]
    out_shape += [jax.ShapeDtypeStruct((t, LANES), F32), jax.ShapeDtypeStruct((LANES, t), F32),
                  jax.ShapeDtypeStruct((b, HALO, CONV_CH), F32)]
    out_specs = [pl.BlockSpec((tm, w), row) for w in widths]
    out_specs += [pl.BlockSpec((tm, LANES), row), pl.BlockSpec((LANES, tm), lambda i: (0, i)),
                  pl.BlockSpec((None, HALO, CONV_CH), bat)]
    return pl.pallas_call(
        functools.partial(_inproj_kernel, per_b=per_b, n_valid=n_valid),
        out_shape=out_shape,
        grid=(t // tm,),
        in_specs=[
            pl.BlockSpec((tm, D_MODEL), row),
            pl.BlockSpec((None, 1, D_MODEL), bat),
            pl.BlockSpec((None, 1, D_MODEL), bat),
            _const((1, D_MODEL), zero),
            pl.BlockSpec((None, HALO, CONV_CH), bat),
        ] + [_const(w.shape, zero) for w in w_list],
        out_specs=out_specs,
        scratch_shapes=[pltpu.VMEM((HALO + tm, CONV_CH), F32)],
        compiler_params=_params(1),
        name="input_projection",
    )(x2, shift, scale, norm_g.reshape(1, -1), conv0, *w_list)


def _bias_kernel(rb_ref, bp_ref, bsc_ref, bsn_ref, *, n_new, past_len):
    tq, tk = ATT_TQ, ATT_TK
    w = tq + tk
    n_tab = rb_ref.shape[1]
    v = lax.broadcasted_iota(jnp.int32, (n_tab, w), 1)
    d = lax.broadcasted_iota(jnp.int32, (n_tab, w), 0)
    idx = jnp.clip(tq - v + WINDOW, -MAX_REL, MAX_REL) + MAX_REL
    onehot = jnp.where(idx == d, 1.0, 0.0).astype(BF16)
    hi, mid, lo = _split3(rb_ref[...])
    base = _dot(hi, onehot) + _dot(mid, onehot) + _dot(lo, onehot)

    chunk_of = lambda pos: jnp.right_shift(pos, CHUNK.bit_length() - 1)
    qi = chunk_of(lax.broadcasted_iota(jnp.int32, (tq, tk), 0))
    kj = chunk_of(lax.broadcasted_iota(jnp.int32, (tq, tk), 1))
    band = (kj >= qi) & (kj <= qi + BAND_CHUNKS)
    sq = chunk_of(past_len + lax.broadcasted_iota(jnp.int32, (n_new, tk), 0))
    sk = chunk_of(past_len - WINDOW + lax.broadcasted_iota(jnp.int32, (n_new, tk), 1))
    sband = (sk <= sq) & (sk >= sq - BAND_CHUNKS)
    for h in range(ATT_HEADS):
        rows = jnp.broadcast_to(base[h:h + 1, :], (tq, w))
        toep = pltpu.roll(rows, 0, 1, stride=1, stride_axis=0)[:, tq:]
        bp_ref[h] = jnp.where(band, toep * LOG2E, NEG)
        srow = jnp.where(sband, toep[:n_new, :], NEG)
        bsc_ref[h] = srow[:, :WINDOW]
        bsn_ref[h] = srow[:, WINDOW:WINDOW + n_new]


def _bias_tables(rel_bias, n_new, past_len):
    n_tab = 2 * MAX_REL + 1
    pad = (-n_tab) % LANES
    rb = jnp.pad(rel_bias, ((0, 0), (0, pad)))
    return pl.pallas_call(
        functools.partial(_bias_kernel, n_new=n_new, past_len=past_len),
        out_shape=[
            jax.ShapeDtypeStruct((ATT_HEADS, ATT_TQ, ATT_TK), F32),
            jax.ShapeDtypeStruct((ATT_HEADS, n_new, WINDOW), F32),
            jax.ShapeDtypeStruct((ATT_HEADS, n_new, n_new), F32),
        ],
        compiler_params=pltpu.CompilerParams(vmem_limit_bytes=VMEM_LIMIT),
        name="relative_bias_tables",
    )(rb)


def _softmax_pv(s_blocks, v_blocks):
    m = s_blocks[0].max(axis=-1, keepdims=True)
    for s in s_blocks[1:]:
        m = jnp.maximum(m, s.max(axis=-1, keepdims=True))
    den = None
    acc = None
    for s, vb in zip(s_blocks, v_blocks):
        e = jnp.exp(s - m)
        part = e.sum(axis=-1, keepdims=True)
        den = part if den is None else den + part
        pv = _dot(e.astype(BF16), vb)
        acc = pv if acc is None else acc + pv
    return acc * (1.0 / den)


def _attn_kernel(q_ref, k0, k1, k2, v0, v1, v2, z_ref, bias_ref, o_ref):
    tq = ATT_TQ
    t = pl.program_id(1)
    first = lax.broadcasted_iota(jnp.int32, (1, LANES), 1) < ATT_HEAD_DIM
    kblocks, vblocks = (k0, k1, k2), (v0, v1, v2)
    n_blocks = len(kblocks)
    n_masked = n_blocks - 1

    def body(masked):
        def stage_scores(h):
            p, a = divmod(h, 2)
            sl = slice(p * LANES, (p + 1) * LANES)
            q2 = q_ref[:, sl].astype(F32) * (ATT_HEAD_DIM ** -0.5 * LOG2E)
            qa = jnp.where(first if a == 0 else ~first, q2, 0.0).astype(BF16)
            s_blocks = []
            for j in range(n_blocks):
                s = _dot_nt(qa, kblocks[j][:, sl]) + bias_ref[h, :, j * tq:(j + 1) * tq]
                if masked and j < n_masked:
                    s = jnp.where(t >= n_masked - j, s, NEG)
                s_blocks.append(s)
            return s_blocks

        def stage_softmax(s_blocks):
            m = s_blocks[0].max(axis=-1, keepdims=True)
            for s in s_blocks[1:]:
                m = jnp.maximum(m, s.max(axis=-1, keepdims=True))
            e_blocks = [jnp.exp2(s - m) for s in s_blocks]
            den = e_blocks[0].sum(axis=-1, keepdims=True)
            for e in e_blocks[1:]:
                den = den + e.sum(axis=-1, keepdims=True)
            return [e.astype(BF16) for e in e_blocks], 1.0 / den

        def stage_pv(h, e_blocks, inv):
            sl = slice((h // 2) * LANES, (h // 2 + 1) * LANES)
            acc = _dot(e_blocks[0], vblocks[0][:, sl])
            for j in range(1, n_blocks):
                acc = acc + _dot(e_blocks[j], vblocks[j][:, sl])
            return acc * inv

        scores, probs, outs = {}, {}, {}
        for n in range(ATT_HEADS + 2):
            if n < ATT_HEADS:
                scores[n] = stage_scores(n)
            if 0 <= n - 1 < ATT_HEADS:
                probs[n - 1] = stage_softmax(scores.pop(n - 1))
            if 0 <= n - 2 < ATT_HEADS:
                h = n - 2
                outs[h] = stage_pv(h, *probs.pop(h))
                if h % 2 == 1:
                    sl = slice((h // 2) * LANES, (h // 2 + 1) * LANES)
                    o2 = jnp.where(first, outs.pop(h - 1), outs.pop(h))
                    o_ref[:, sl] = (o2 * z_ref[:, sl].astype(F32)).astype(o_ref.dtype)

    @pl.when(t < n_masked)
    def _():
        body(True)

    @pl.when(t >= n_masked)
    def _():
        body(False)


def _prompt_attention(q, k, v, za, bias, b, s):
    tq = ATT_TQ
    nt = s // tq
    cur = lambda i, t: (i * nt + t, 0)
    prev1 = lambda i, t: (i * nt + jnp.maximum(t - 1, 0), 0)
    prev2 = lambda i, t: (i * nt + jnp.maximum(t - 2, 0), 0)
    blk = lambda m: pl.BlockSpec((tq, ATT_WIDTH), m)
    return pl.pallas_call(
        _attn_kernel,
        out_shape=jax.ShapeDtypeStruct((b * s, ATT_WIDTH), BF16),
        grid=(b, nt),
        in_specs=[blk(cur), blk(prev2), blk(prev1), blk(cur), blk(prev2), blk(prev1), blk(cur), blk(cur),
                  _const((ATT_HEADS, tq, ATT_TK), lambda i, t: (0, 0, 0))],
        out_specs=blk(cur),
        compiler_params=_params(2),
        name="prompt_band_attention",
    )(q, k, k, k, v, v, v, za, bias)


def _sample_attn_kernel(q_ref, kc_ref, vc_ref, kn_ref, vn_ref, z_ref, bc_ref, bn_ref, o_ref):
    scale = ATT_HEAD_DIM ** -0.5
    n_new = q_ref.shape[0]
    o_ref[n_new:, :] = jnp.zeros((o_ref.shape[0] - n_new, o_ref.shape[1]), o_ref.dtype)
    lane = lax.broadcasted_iota(jnp.int32, (1, LANES), 1)
    first = lane < ATT_HEAD_DIM
    for p in range(ATT_WIDTH // LANES):
        sl = slice(p * LANES, (p + 1) * LANES)
        q2 = q_ref[:, sl]
        kb = [kc_ref[:, sl].astype(BF16), kn_ref[:, sl]]
        vb = [vc_ref[:, sl].astype(BF16), vn_ref[:, sl]]
        outs = []
        for a in range(2):
            qa = jnp.where(first if a == 0 else ~first, q2, jnp.zeros_like(q2))
            h = 2 * p + a
            s_blocks = [_dot_nt(qa, kb[0]) * scale + bc_ref[h], _dot_nt(qa, kb[1]) * scale + bn_ref[h]]
            outs.append(_softmax_pv(s_blocks, vb))
        o2 = jnp.where(first, outs[0], outs[1])
        o_ref[:n_new, sl] = (o2 * z_ref[:, sl].astype(F32)).astype(o_ref.dtype)


def _sample_attention(q, k, v, za, cache_k, cache_v, bias_c, bias_n, nb, n_new, s_pad):
    step = s_pad // n_new
    new = pl.BlockSpec((n_new, ATT_WIDTH), lambda i: (i * step, 0))
    old = pl.BlockSpec((None, WINDOW, ATT_WIDTH), lambda i: (i, 0, 0))
    return pl.pallas_call(
        _sample_attn_kernel,
        out_shape=jax.ShapeDtypeStruct((nb * s_pad, ATT_WIDTH), BF16),
        grid=(nb,),
        in_specs=[new, old, old, new, new, new,
                  _const(bias_c.shape, lambda i: (0, 0, 0)), _const(bias_n.shape, lambda i: (0, 0, 0))],
        out_specs=pl.BlockSpec((s_pad, ATT_WIDTH), lambda i: (i, 0)),
        compiler_params=_params(1),
        name="sample_band_attention",
    )(q, cache_k, cache_v, k, v, za, bias_c, bias_n)


def _ssd_kernel(*refs, chunk, n_valid, has_init):
    if has_init:
        (xc_ref, dt_ref, dtT_ref, dtb_ref, dtbT_ref, alog_ref, alogT_ref, dsk_ref, h0_ref,
         y_ref, hout_ref, xw, hT) = refs
    else:
        (xc_ref, dt_ref, dtT_ref, dtb_ref, dtbT_ref, alog_ref, alogT_ref, dsk_ref,
         y_ref, hout_ref, xw, hT) = refs
    L = chunk
    c = pl.program_id(1)
    n_state = SSD_GROUPS * SSD_STATE

    @pl.when(c == 0)
    def _():
        if has_init:
            for g in range(SSD_GROUPS):
                hT[g] = h0_ref[g * GROUP_W:(g + 1) * GROUP_W, :].T
        else:
            hT[...] = jnp.zeros(hT.shape, F32)

    dt = _softplus(dt_ref[...] + dtb_ref[...])
    dtT = _softplus(dtT_ref[...] + dtbT_ref[...])
    if n_valid < L:
        dt = jnp.where(lax.broadcasted_iota(jnp.int32, dt.shape, 0) < n_valid, dt, 0.0)
        dtT = jnp.where(lax.broadcasted_iota(jnp.int32, dtT.shape, 1) < n_valid, dtT, 0.0)
    da = dt * (-jnp.exp(alog_ref[...]))
    daT = dtT * (-jnp.exp(alogT_ref[...]))
    ri = lax.broadcasted_iota(jnp.int32, (L, L), 0)
    ci = lax.broadcasted_iota(jnp.int32, (L, L), 1)
    causal = ri >= ci
    lower = jnp.where(causal, 1.0, 0.0).astype(BF16)
    upper = jnp.where(ri <= ci, 1.0, 0.0).astype(BF16)
    acs = sum(_dot(lower, part) for part in _split3(da))
    acsT = sum(_dot(part, upper) for part in _split3(daT))
    acs_last = acs[L - 1:L, :]
    e_acs = jnp.exp(acs)
    w_end = dt * jnp.exp(acs_last - acs)
    e_last = jnp.exp(acs_last)
    col_term = acs * LOG2E
    row_term = (acsT - jnp.log(dtT)) * LOG2E

    first = lax.broadcasted_iota(jnp.int32, (1, LANES), 1) < SSD_HEAD_DIM

    for g in range(SSD_GROUPS):
        b_g = xc_ref[:, SSD_WIDTH + g * SSD_STATE:SSD_WIDTH + (g + 1) * SSD_STATE]
        c_g = xc_ref[:, SSD_WIDTH + n_state + g * SSD_STATE:SSD_WIDTH + n_state + (g + 1) * SSD_STATE]
        cb = _dot_nt(c_g, b_g)
        h_old = hT[g]
        y_state = _dot(c_g, h_old.astype(BF16))
        decays = []
        for pp in range(HEADS_PER_GROUP // 2):
            e0 = g * HEADS_PER_GROUP + 2 * pp
            col0 = g * GROUP_W + pp * LANES
            xp_b = xc_ref[:, col0:col0 + LANES]
            xp = xp_b.astype(F32)
            y = y_state[:, pp * LANES:(pp + 1) * LANES] * _pair_cols(e_acs, e0, L, first)
            for a in range(2):
                e = e0 + a
                seg = jnp.broadcast_to(col_term[:, e:e + 1], (L, L)) - row_term[e:e + 1, :]
                m = cb * jnp.exp2(jnp.where(causal, seg, NEG))
                xa = jnp.where(first if a == 0 else ~first, xp_b, jnp.zeros_like(xp_b))
                y = y + _dot(m.astype(BF16), xa)
            y_ref[:, col0:col0 + LANES] = (y + dsk_ref[:, col0:col0 + LANES] * xp).astype(y_ref.dtype)
            xw[:, pp * LANES:(pp + 1) * LANES] = (xp * _pair_cols(w_end, e0, L, first)).astype(xw.dtype)
            decays.append(_pair_cols(e_last, e0, 1, first))
        b_gt = b_g.astype(F32).T.astype(BF16)
        hT[g] = h_old * jnp.concatenate(decays, axis=1) + _dot(b_gt, xw[...])

    @pl.when(c == pl.num_programs(1) - 1)
    def _():
        for g in range(SSD_GROUPS):
            hout_ref[g * GROUP_W:(g + 1) * GROUP_W, :] = hT[g].T


def _ssd(xc, dt, dtT, wts, b, s, chunk, n_valid, h0=None):
    nc = s // chunk
    has_init = h0 is not None
    row = lambda i, c: (i * nc + c, 0)
    zero = lambda i, c: (0, 0)
    per_b = lambda i, c: (i, 0, 0)
    names = ("dt_b", "dt_bT", "a_log", "a_logT", "d_skip")
    w_list = [wts[n] for n in names]
    in_specs = [
        pl.BlockSpec((chunk, CONV_CH), row),
        pl.BlockSpec((chunk, LANES), row),
        pl.BlockSpec((LANES, chunk), lambda i, c: (0, i * nc + c)),
    ] + [_const(w.shape, zero) for w in w_list]
    args = [xc, dt, dtT] + w_list
    if has_init:
        in_specs += [pl.BlockSpec((None, SSD_HEADS * SSD_HEAD_DIM, SSD_STATE), per_b)]
        args += [h0]
    return pl.pallas_call(
        functools.partial(_ssd_kernel, chunk=chunk, n_valid=n_valid, has_init=has_init),
        out_shape=[jax.ShapeDtypeStruct((b * s, SSD_WIDTH), BF16),
                   jax.ShapeDtypeStruct((b, SSD_HEADS * SSD_HEAD_DIM, SSD_STATE), F32)],
        grid=(b, nc),
        in_specs=in_specs,
        out_specs=[pl.BlockSpec((chunk, SSD_WIDTH), row),
                   pl.BlockSpec((None, SSD_HEADS * SSD_HEAD_DIM, SSD_STATE), per_b)],
        scratch_shapes=[
            pltpu.VMEM((chunk, GROUP_W), BF16),
            pltpu.VMEM((SSD_GROUPS, SSD_STATE, GROUP_W), F32),
        ],
        compiler_params=_params(2),
        name="ssd_scan",
    )(*args)


def _out_kernel(x_ref, gate_ref, att_ref, y_ref, zs_ref, g_ref, ng_ref, wap, wsp, wout, o_ref):
    att = _dot(att_ref[...], wap[...])
    ssd = None
    for g in range(SSD_GROUPS):
        cols = slice(g * GROUP_W, (g + 1) * GROUP_W)
        yg = y_ref[:, cols].astype(F32) * zs_ref[:, cols].astype(F32)
        ms = jnp.mean(yg * yg, axis=-1, keepdims=True)
        yn = (yg * lax.rsqrt(ms + EPS) * ng_ref[:, cols]).astype(BF16)
        part = _dot(yn, wsp[cols, :])
        ssd = part if ssd is None else ssd + part
    gates = g_ref[...].astype(F32)
    merged = gates[:, :D_MODEL] * att + gates[:, D_MODEL:] * ssd
    o_ref[...] = x_ref[...] + gate_ref[...] * _dot(merged.astype(BF16), wout[...])


def _output(x, gate, att, y, zs, gates, wts, tm):
    b, s, _ = x.shape
    t = b * s
    per_b = s // tm
    row = lambda i: (i, 0)
    zero = lambda i: (0, 0)
    w_list = [wts["ssd_ng"], wts["w_att_proj"], wts["w_ssd_proj"], wts["w_out"]]
    out = pl.pallas_call(
        _out_kernel,
        out_shape=jax.ShapeDtypeStruct((t, D_MODEL), F32),
        grid=(t // tm,),
        in_specs=[
            pl.BlockSpec((tm, D_MODEL), row),
            pl.BlockSpec((None, 1, D_MODEL), lambda i: (i // per_b, 0, 0)),
            pl.BlockSpec((tm, ATT_WIDTH), row),
            pl.BlockSpec((tm, SSD_WIDTH), row),
            pl.BlockSpec((tm, SSD_WIDTH), row),
            pl.BlockSpec((tm, 2 * D_MODEL), row),
        ] + [_const(w.shape, zero) for w in w_list],
        out_specs=pl.BlockSpec((tm, D_MODEL), row),
        compiler_params=_params(1),
        name="merge_output_projection",
    )(x.reshape(t, D_MODEL), gate, att, y, zs, gates, *w_list)
    return out.reshape(b, s, D_MODEL)


def _layer_weights(w_in, q_norm_g, k_norm_g, w_att_proj, conv_w, conv_b, dt_bias, a_log, d_skip, ssd_norm_g,
                   w_ssd_proj, w_out):
    sizes = (ATT_WIDTH, ATT_WIDTH, ATT_WIDTH, ATT_WIDTH, SSD_WIDTH, CONV_CH, SSD_HEADS, 2 * D_MODEL)
    offs = [0]
    for n in sizes:
        offs.append(offs[-1] + n)
    wq, wk, wv, wza, wzs, wxbc, wdt, wg = (w_in[:, offs[i]:offs[i + 1]].astype(BF16) for i in range(len(sizes)))
    pad_h = LANES - SSD_HEADS
    head_of = jnp.arange(ATT_WIDTH) // ATT_HEAD_DIM
    sel = head_of[:, None] == jnp.arange(LANES)[None, :]
    row_vec = lambda v: jnp.pad(v.astype(F32), (0, pad_h)).reshape(1, LANES)
    return {
        "wq": wq, "wk": wk, "wv": wv, "wza": wza, "wzs": wzs, "wxbc": wxbc, "wg": wg,
        "wdt": jnp.pad(wdt, ((0, 0), (0, pad_h))),
        "wdtT": jnp.pad(wdt.T, ((0, pad_h), (0, 0))),
        "gsum": jnp.where(sel, 1.0 / ATT_HEAD_DIM, 0.0).astype(BF16),
        "qg": jnp.tile(q_norm_g.astype(F32), ATT_HEADS).reshape(1, ATT_WIDTH),
        "kg": jnp.tile(k_norm_g.astype(F32), ATT_HEADS).reshape(1, ATT_WIDTH),
        "conv_w": conv_w.astype(F32), "conv_b": conv_b.astype(F32).reshape(1, CONV_CH),
        "dt_b": row_vec(dt_bias), "dt_bT": row_vec(dt_bias).reshape(LANES, 1),
        "a_log": row_vec(a_log), "a_logT": row_vec(a_log).reshape(LANES, 1),
        "d_skip": jnp.repeat(d_skip.astype(F32), SSD_HEAD_DIM).reshape(1, SSD_WIDTH),
        "ssd_ng": ssd_norm_g.astype(F32).reshape(1, SSD_WIDTH),
        "w_att_proj": w_att_proj.astype(BF16), "w_ssd_proj": w_ssd_proj.astype(BF16), "w_out": w_out.astype(BF16),
    }


def _trunk_layer(x, mod, norm_g, wts, attn_fn, conv0, chunk, n_valid, h0=None):
    b, s, _ = x.shape
    shift, scale, gate = (mod[:, :, i * D_MODEL:(i + 1) * D_MODEL] for i in range(3))
    tm = min(IN_TM, s)
    q, k, v, za, zs, xc, gates, dt, dtT, tail = _input_projection(x, shift, scale, norm_g, conv0, wts, tm,
                                                                 min(n_valid, tm))
    att = attn_fn(q, k, v, za)
    y, h_new = _ssd(xc, dt, dtT, wts, b, s, chunk, n_valid, h0)
    out = _output(x, gate, att, y, zs, gates, wts, min(OUT_TM, s))
    return out, k.reshape(b, s, ATT_WIDTH), v.reshape(b, s, ATT_WIDTH), tail[:, HALO - (CONV_WIDTH - 1):], h_new


def kernel(x_prompt, x_sample, c_prompt, c_sample, cache_k, cache_v, state_conv, state_ssm, norm_g, w_ada, b_ada, w_in, q_norm_g, k_norm_g, rel_bias, w_att_proj, conv_w, conv_b, dt_bias, a_log, d_skip, ssd_norm_g, w_ssd_proj, w_out):
    depth = w_in.shape[0]
    bp, sp, _ = x_prompt.shape
    bs, n_new, _ = x_sample.shape
    rows = cache_k.shape[2]
    assert rows == WINDOW and n_new <= CHUNK and n_new % SUBLANES == 0 and sp % SSD_L == 0 and sp % ATT_TQ == 0
    heads = (ATT_HEADS, ATT_HEAD_DIM)
    ssm_shape = (SSD_HEADS, SSD_HEAD_DIM, SSD_STATE)
    n_c = bp + bs
    c_all = jnp.pad(jnp.concatenate([c_prompt, c_sample], axis=0), ((0, (-n_c) % SUBLANES), (0, 0)))
    xp = x_prompt
    xs = jnp.pad(x_sample, ((0, 0), (0, SAMPLE_PAD - n_new), (0, 0)))
    outs = [[] for _ in range(8)]
    for l in range(depth):
        wts = _layer_weights(w_in[l], q_norm_g[l], k_norm_g[l], w_att_proj[l], conv_w[l], conv_b[l], dt_bias[l],
                             a_log[l], d_skip[l], ssd_norm_g[l], w_ssd_proj[l], w_out[l])
        mod = _modulation(c_all, w_ada[l], b_ada[l])[:, None, :]
        bias_p, bias_c, bias_n = _bias_tables(rel_bias[l], n_new, PAST_LEN)

        attn_p = functools.partial(_prompt_attention, bias=bias_p, b=bp, s=sp)
        conv0_p = jnp.zeros((bp, HALO, CONV_CH), F32)
        xp, kp, vp, cp, hp = _trunk_layer(xp, mod[:bp], norm_g[l], wts, attn_p, conv0_p, SSD_L, SSD_L)

        attn_s = functools.partial(
            _sample_attention, cache_k=cache_k[l].reshape(bs, rows, ATT_WIDTH),
            cache_v=cache_v[l].reshape(bs, rows, ATT_WIDTH), bias_c=bias_c, bias_n=bias_n,
            nb=bs, n_new=n_new, s_pad=SAMPLE_PAD)
        conv0_s = jnp.pad(state_conv[l].astype(F32), ((0, 0), (HALO - (CONV_WIDTH - 1), 0), (0, 0)))
        h0 = state_ssm[l].reshape(bs, SSD_HEADS * SSD_HEAD_DIM, SSD_STATE)
        xs, ks, vs, cs, hs = _trunk_layer(xs, mod[bp:n_c], norm_g[l], wts, attn_s, conv0_s, SAMPLE_PAD, n_new, h0)

        keep = min(WINDOW, sp)
        outs[0].append(kp[:, sp - keep:].astype(F32).reshape(bp, keep, *heads))
        outs[1].append(vp[:, sp - keep:].astype(F32).reshape(bp, keep, *heads))
        outs[2].append(cp)
        outs[3].append(hp.reshape(bp, *ssm_shape))
        outs[4].append(ks[:, :n_new].astype(F32).reshape(bs, n_new, *heads))
        outs[5].append(vs[:, :n_new].astype(F32).reshape(bs, n_new, *heads))
        outs[6].append(cs)
        outs[7].append(hs.reshape(bs, *ssm_shape))
    return (xp, xs[:, :n_new]) + tuple(jnp.stack(o) for o in outs)
```

```python
import functools

import jax
import jax.numpy as jnp
from jax import lax
from jax.experimental import pallas as pl
from jax.experimental.pallas import tpu as pltpu

F32 = jnp.float32
BF16 = jnp.bfloat16

D_MODEL = 1024
CHUNK = 64
BAND_CHUNKS = 8
WINDOW = BAND_CHUNKS * CHUNK
ATT_HEADS = 16
ATT_HEAD_DIM = 64
ATT_WIDTH = ATT_HEADS * ATT_HEAD_DIM
MAX_REL = 256
SSD_WIDTH = 2 * D_MODEL
SSD_HEAD_DIM = 64
SSD_HEADS = SSD_WIDTH // SSD_HEAD_DIM
SSD_GROUPS = 4
SSD_STATE = 128
CONV_WIDTH = 4
CONV_CH = SSD_WIDTH + 2 * SSD_GROUPS * SSD_STATE
PAST_LEN = 4096
EPS = 1e-6
NEG = -1e30
LOG2E = 1.4426950408889634

LANES = 128
SUBLANES = 8
VMEM_LIMIT = 56 * 1024 * 1024

IN_TM = 256
CONV_BLOCKS = 6
OUT_TM = 512
ATT_TQ = 256
ATT_TK = ATT_TQ + WINDOW
SSD_L = 256
SAMPLE_PAD = 128
HEADS_PER_GROUP = SSD_HEADS // SSD_GROUPS
GROUP_W = SSD_WIDTH // SSD_GROUPS
HALO = SUBLANES


def _const(shape, index_map):
    return pl.BlockSpec(shape, index_map, pipeline_mode=pl.Buffered(1))


def _params(n_axes):
    return pltpu.CompilerParams(dimension_semantics=("arbitrary",) * n_axes, vmem_limit_bytes=VMEM_LIMIT)


def _dot(a, b):
    return jnp.dot(a, b, preferred_element_type=F32)


def _dot_nt(a, b):
    return lax.dot_general(a, b, (((1,), (1,)), ((), ())), preferred_element_type=F32)


def _split3(x):
    hi = x.astype(BF16)
    r1 = x - hi.astype(F32)
    mid = r1.astype(BF16)
    lo = (r1 - mid.astype(F32)).astype(BF16)
    return hi, mid, lo


def _sigmoid(x):
    return 0.5 * jnp.tanh(0.5 * x) + 0.5


def _silu(x):
    h = 0.5 * x
    return h * jnp.tanh(h) + h


def _softplus(x):
    return jnp.maximum(x, 0.0) + jnp.log1p(jnp.exp(-jnp.abs(x)))


def _mod_kernel(c_ref, w_ref, b_ref, o_ref):
    c = c_ref[...]
    a = (c * jax.nn.sigmoid(c)).astype(BF16)
    o_ref[...] = _dot(a, w_ref[...].astype(BF16)) + b_ref[...]


def _modulation(c_all, w_ada, b_ada):
    n = c_all.shape[0]
    return pl.pallas_call(
        _mod_kernel,
        out_shape=jax.ShapeDtypeStruct((n, 3 * D_MODEL), F32),
        grid=(3,),
        in_specs=[
            pl.BlockSpec((n, D_MODEL), lambda j: (0, 0)),
            pl.BlockSpec((D_MODEL, D_MODEL), lambda j: (0, j)),
            pl.BlockSpec((1, D_MODEL), lambda j: (0, j)),
        ],
        out_specs=pl.BlockSpec((n, D_MODEL), lambda j: (0, j)),
        compiler_params=_params(1),
        name="adaln_modulation",
    )(c_all, w_ada, b_ada.reshape(1, -1))


def _pair_cols(mat, e0, rows, first):
    a = jnp.broadcast_to(mat[:, e0:e0 + 1], (rows, LANES))
    b = jnp.broadcast_to(mat[:, e0 + 1:e0 + 2], (rows, LANES))
    return jnp.where(first, a, b)


def _inproj_kernel(x_ref, shift_ref, scale_ref, ng_ref, conv0_ref, wq, wk, wv, wza, wzs, wxbc, wg, wdt, wdtT,
                   gsum, qg, kg, cw_ref, cb_ref, q_o, k_o, v_o, za_o, zs_o, xc_o, g_o, dt_o, dtT_o, tail_o, *xpads,
                   per_b, n_valid):
    tm = x_ref.shape[0]
    cblk = CONV_CH // len(xpads)
    lag = [CONV_WIDTH - 1 - j for j in range(CONV_WIDTH)]

    @pl.when(lax.rem(pl.program_id(0), per_b) == 0)
    def _():
        for i, xpad in enumerate(xpads):
            for j in range(CONV_WIDTH - 1):
                xpad[j, HALO:HALO + lag[j], :] = conv0_ref[HALO - lag[j]:HALO, i * cblk:(i + 1) * cblk]

    x = x_ref[...]
    ms = jnp.mean(x * x, axis=-1, keepdims=True)
    h = x * lax.rsqrt(ms + EPS) * ng_ref[...]
    h = h * (1.0 + scale_ref[...]) + shift_ref[...]
    hb = h.astype(BF16)
    first = lax.broadcasted_iota(jnp.int32, (1, LANES), 1) < ATT_HEAD_DIM

    def head_norm(y, g_ref):
        ss = _dot((y * y).astype(BF16), gsum[...])
        r = lax.rsqrt(ss + EPS)
        parts = [y[:, p * LANES:(p + 1) * LANES] * _pair_cols(r, 2 * p, tm, first)
                 for p in range(ATT_WIDTH // LANES)]
        return jnp.concatenate(parts, axis=1) * g_ref[...]

    def conv_matmul(i):
        raw = _dot(hb, wxbc[:, i * cblk:(i + 1) * cblk])
        for j in range(CONV_WIDTH):
            xpads[i][j, HALO + lag[j]:HALO + lag[j] + tm, :] = raw

    def conv_silu(i):
        xpad, cols = xpads[i], slice(i * cblk, (i + 1) * cblk)
        acc = cb_ref[:, cols]
        for j in range(CONV_WIDTH):
            acc = acc + xpad[j, HALO:HALO + tm, :] * cw_ref[j:j + 1, cols]
        xc_o[:, cols] = _silu(acc).astype(xc_o.dtype)
        tail_o[:, cols] = xpad[CONV_WIDTH - 1, n_valid:n_valid + HALO, :]
        for j in range(CONV_WIDTH - 1):
            xpad[j, HALO:HALO + lag[j], :] = xpad[j, HALO + tm:HALO + tm + lag[j], :]

    half = SSD_WIDTH // 2
    projections = [
        (lambda: _dot(hb, wq[...]), lambda y: q_o.__setitem__(..., head_norm(y, qg).astype(q_o.dtype))),
        (lambda: _dot(hb, wk[...]), lambda y: k_o.__setitem__(..., head_norm(y, kg).astype(k_o.dtype))),
        (lambda: _dot(hb, wv[...]), lambda y: v_o.__setitem__(..., y.astype(v_o.dtype))),
        (lambda: _dot(hb, wza[...]), lambda y: za_o.__setitem__(..., _silu(y).astype(za_o.dtype))),
    ]
    for c0 in (0, half):
        cols = slice(c0, c0 + half)
        projections.append((lambda cols=cols: _dot(hb, wzs[:, cols]),
                            lambda y, cols=cols: zs_o.__setitem__((slice(None), cols), _silu(y).astype(zs_o.dtype))))
    for c0 in (0, half):
        cols = slice(c0, c0 + half)
        projections.append((lambda cols=cols: _dot(hb, wg[:, cols]),
                            lambda y, cols=cols: g_o.__setitem__((slice(None), cols), _sigmoid(y).astype(g_o.dtype))))

    n_conv = len(xpads)
    pending = {}
    for step in range(max(n_conv, len(projections)) + 1):
        if step < n_conv:
            conv_matmul(step)
        if step < len(projections):
            pending[step] = projections[step][0]()
        if 0 <= step - 1 < n_conv:
            conv_silu(step - 1)
        if 0 <= step - 1 < len(projections):
            projections[step - 1][1](pending.pop(step - 1))
    dt_o[...] = _dot(hb, wdt[...])
    dtT_o[...] = _dot_nt(wdtT[...], hb)


def _input_projection(x, shift, scale, norm_g, conv0, wts, tm, n_valid):
    b, s, _ = x.shape
    t = b * s
    per_b = s // tm
    x2 = x.reshape(t, D_MODEL)
    row = lambda i: (i, 0)
    bat = lambda i: (i // per_b, 0, 0)
    zero = lambda i: (0, 0)
    w_names = ("wq", "wk", "wv", "wza", "wzs", "wxbc", "wg", "wdt", "wdtT", "gsum", "qg", "kg", "conv_w", "conv_b")
    w_list = [wts[n] for n in w_names]
    widths = (ATT_WIDTH, ATT_WIDTH, ATT_WIDTH, ATT_WIDTH, SSD_WIDTH, CONV_CH, 2 * D_MODEL)
    out_shape = [jax.ShapeDtypeStruct((t, w), BF16) for w in widths]
    out_shape += [jax.ShapeDtypeStruct((t, LANES), F32), jax.ShapeDtypeStruct((LANES, t), F32),
                  jax.ShapeDtypeStruct((b, HALO, CONV_CH), F32)]
    out_specs = [pl.BlockSpec((tm, w), row) for w in widths]
    out_specs += [pl.BlockSpec((tm, LANES), row), pl.BlockSpec((LANES, tm), lambda i: (0, i)),
                  pl.BlockSpec((None, HALO, CONV_CH), bat)]
    return pl.pallas_call(
        functools.partial(_inproj_kernel, per_b=per_b, n_valid=n_valid),
        out_shape=out_shape,
        grid=(t // tm,),
        in_specs=[
            pl.BlockSpec((tm, D_MODEL), row),
            pl.BlockSpec((None, 1, D_MODEL), bat),
            pl.BlockSpec((None, 1, D_MODEL), bat),
            _const((1, D_MODEL), zero),
            pl.BlockSpec((None, HALO, CONV_CH), bat),
        ] + [_const(w.shape, zero) for w in w_list],
        out_specs=out_specs,
        scratch_shapes=[pltpu.VMEM((CONV_WIDTH, HALO + tm + HALO, CONV_CH // CONV_BLOCKS), F32)
                        for _ in range(CONV_BLOCKS)],
        compiler_params=_params(1),
        name="input_projection",
    )(x2, shift, scale, norm_g.reshape(1, -1), conv0, *w_list)


def _bias_kernel(rb_ref, bp_ref, bsc_ref, bsn_ref, *, n_new, past_len):
    tq, tk = ATT_TQ, ATT_TK
    w = tq + tk
    n_tab = rb_ref.shape[1]
    v = lax.broadcasted_iota(jnp.int32, (n_tab, w), 1)
    d = lax.broadcasted_iota(jnp.int32, (n_tab, w), 0)
    idx = jnp.clip(tq - v + WINDOW, -MAX_REL, MAX_REL) + MAX_REL
    onehot = jnp.where(idx == d, 1.0, 0.0).astype(BF16)
    hi, mid, lo = _split3(rb_ref[...])
    base = _dot(hi, onehot) + _dot(mid, onehot) + _dot(lo, onehot)

    chunk_of = lambda pos: jnp.right_shift(pos, CHUNK.bit_length() - 1)
    qi = chunk_of(lax.broadcasted_iota(jnp.int32, (tq, tk), 0))
    kj = chunk_of(lax.broadcasted_iota(jnp.int32, (tq, tk), 1))
    band = (kj >= qi) & (kj <= qi + BAND_CHUNKS)
    sq = chunk_of(past_len + lax.broadcasted_iota(jnp.int32, (n_new, tk), 0))
    sk = chunk_of(past_len - WINDOW + lax.broadcasted_iota(jnp.int32, (n_new, tk), 1))
    sband = (sk <= sq) & (sk >= sq - BAND_CHUNKS)
    for h in range(ATT_HEADS):
        rows = jnp.broadcast_to(base[h:h + 1, :], (tq, w))
        toep = pltpu.roll(rows, 0, 1, stride=1, stride_axis=0)[:, tq:]
        bp_ref[h] = jnp.where(band, toep * LOG2E, NEG)
        srow = jnp.where(sband, toep[:n_new, :], NEG)
        bsc_ref[h] = srow[:, :WINDOW]
        bsn_ref[h] = srow[:, WINDOW:WINDOW + n_new]


def _bias_tables(rel_bias, n_new, past_len):
    n_tab = 2 * MAX_REL + 1
    pad = (-n_tab) % LANES
    rb = jnp.pad(rel_bias, ((0, 0), (0, pad)))
    return pl.pallas_call(
        functools.partial(_bias_kernel, n_new=n_new, past_len=past_len),
        out_shape=[
            jax.ShapeDtypeStruct((ATT_HEADS, ATT_TQ, ATT_TK), F32),
            jax.ShapeDtypeStruct((ATT_HEADS, n_new, WINDOW), F32),
            jax.ShapeDtypeStruct((ATT_HEADS, n_new, n_new), F32),
        ],
        compiler_params=pltpu.CompilerParams(vmem_limit_bytes=VMEM_LIMIT),
        name="relative_bias_tables",
    )(rb)


def _softmax_pv(s_blocks, v_blocks):
    m = s_blocks[0].max(axis=-1, keepdims=True)
    for s in s_blocks[1:]:
        m = jnp.maximum(m, s.max(axis=-1, keepdims=True))
    den = None
    acc = None
    for s, vb in zip(s_blocks, v_blocks):
        e = jnp.exp(s - m)
        part = e.sum(axis=-1, keepdims=True)
        den = part if den is None else den + part
        pv = _dot(e.astype(BF16), vb)
        acc = pv if acc is None else acc + pv
    return acc * (1.0 / den)


def _attn_kernel(q_ref, k0, k1, k2, v0, v1, v2, z_ref, bias_ref, o_ref):
    tq = ATT_TQ
    t = pl.program_id(1)
    first = lax.broadcasted_iota(jnp.int32, (1, LANES), 1) < ATT_HEAD_DIM
    kblocks, vblocks = (k0, k1, k2), (v0, v1, v2)
    n_blocks = len(kblocks)
    n_masked = n_blocks - 1

    def body(masked):
        def stage_scores(h):
            p, a = divmod(h, 2)
            sl = slice(p * LANES, (p + 1) * LANES)
            q2 = q_ref[:, sl].astype(F32) * (ATT_HEAD_DIM ** -0.5 * LOG2E)
            qa = jnp.where(first if a == 0 else ~first, q2, 0.0).astype(BF16)
            s_blocks = []
            for j in range(n_blocks):
                s = _dot_nt(qa, kblocks[j][:, sl]) + bias_ref[h, :, j * tq:(j + 1) * tq]
                if masked and j < n_masked:
                    s = jnp.where(t >= n_masked - j, s, NEG)
                s_blocks.append(s)
            return s_blocks

        def stage_softmax(s_blocks):
            m = s_blocks[0].max(axis=-1, keepdims=True)
            for s in s_blocks[1:]:
                m = jnp.maximum(m, s.max(axis=-1, keepdims=True))
            e_blocks = [jnp.exp2(s - m) for s in s_blocks]
            den = e_blocks[0].sum(axis=-1, keepdims=True)
            for e in e_blocks[1:]:
                den = den + e.sum(axis=-1, keepdims=True)
            return [e.astype(BF16) for e in e_blocks], 1.0 / den

        def stage_pv(h, e_blocks, inv):
            sl = slice((h // 2) * LANES, (h // 2 + 1) * LANES)
            acc = _dot(e_blocks[0], vblocks[0][:, sl])
            for j in range(1, n_blocks):
                acc = acc + _dot(e_blocks[j], vblocks[j][:, sl])
            return acc * inv

        scores, probs, outs = {}, {}, {}
        for n in range(ATT_HEADS + 2):
            if n < ATT_HEADS:
                scores[n] = stage_scores(n)
            if 0 <= n - 1 < ATT_HEADS:
                probs[n - 1] = stage_softmax(scores.pop(n - 1))
            if 0 <= n - 2 < ATT_HEADS:
                h = n - 2
                outs[h] = stage_pv(h, *probs.pop(h))
                if h % 2 == 1:
                    sl = slice((h // 2) * LANES, (h // 2 + 1) * LANES)
                    o2 = jnp.where(first, outs.pop(h - 1), outs.pop(h))
                    o_ref[:, sl] = (o2 * z_ref[:, sl].astype(F32)).astype(o_ref.dtype)

    @pl.when(t < n_masked)
    def _():
        body(True)

    @pl.when(t >= n_masked)
    def _():
        body(False)


def _prompt_attention(q, k, v, za, bias, b, s):
    tq = ATT_TQ
    nt = s // tq
    cur = lambda i, t: (i * nt + t, 0)
    prev1 = lambda i, t: (i * nt + jnp.maximum(t - 1, 0), 0)
    prev2 = lambda i, t: (i * nt + jnp.maximum(t - 2, 0), 0)
    blk = lambda m: pl.BlockSpec((tq, ATT_WIDTH), m)
    return pl.pallas_call(
        _attn_kernel,
        out_shape=jax.ShapeDtypeStruct((b * s, ATT_WIDTH), BF16),
        grid=(b, nt),
        in_specs=[blk(cur), blk(prev2), blk(prev1), blk(cur), blk(prev2), blk(prev1), blk(cur), blk(cur),
                  _const((ATT_HEADS, tq, ATT_TK), lambda i, t: (0, 0, 0))],
        out_specs=blk(cur),
        compiler_params=_params(2),
        name="prompt_band_attention",
    )(q, k, k, k, v, v, v, za, bias)


def _sample_attn_kernel(q_ref, kc_ref, vc_ref, kn_ref, vn_ref, z_ref, bc_ref, bn_ref, o_ref):
    scale = ATT_HEAD_DIM ** -0.5
    n_new = q_ref.shape[0]
    o_ref[n_new:, :] = jnp.zeros((o_ref.shape[0] - n_new, o_ref.shape[1]), o_ref.dtype)
    for p in range(ATT_WIDTH // LANES):
        sl = slice(p * LANES, (p + 1) * LANES)
        q2, kn2, vn2 = q_ref[:, sl], kn_ref[:, sl], vn_ref[:, sl]
        outs = []
        for a in range(2):
            h = 2 * p + a
            hs = slice(a * ATT_HEAD_DIM, (a + 1) * ATT_HEAD_DIM)
            kc = kc_ref[pl.ds(h, WINDOW, stride=ATT_HEADS), :].astype(BF16)
            vc = vc_ref[pl.ds(h, WINDOW, stride=ATT_HEADS), :].astype(BF16)
            qh = q2[:, hs]
            s_blocks = [_dot_nt(qh, kc) * scale + bc_ref[h], _dot_nt(qh, kn2[:, hs]) * scale + bn_ref[h]]
            outs.append(_softmax_pv(s_blocks, [vc, vn2[:, hs]]))
        o2 = jnp.concatenate(outs, axis=1)
        o_ref[:n_new, sl] = (o2 * z_ref[:, sl].astype(F32)).astype(o_ref.dtype)


def _sample_attention(q, k, v, za, cache_k, cache_v, bias_c, bias_n, nb, n_new, s_pad):
    step = s_pad // n_new
    new = pl.BlockSpec((n_new, ATT_WIDTH), lambda i: (i * step, 0))
    old = pl.BlockSpec((None, WINDOW * ATT_HEADS, ATT_HEAD_DIM), lambda i: (i, 0, 0))
    return pl.pallas_call(
        _sample_attn_kernel,
        out_shape=jax.ShapeDtypeStruct((nb * s_pad, ATT_WIDTH), BF16),
        grid=(nb,),
        in_specs=[new, old, old, new, new, new,
                  _const(bias_c.shape, lambda i: (0, 0, 0)), _const(bias_n.shape, lambda i: (0, 0, 0))],
        out_specs=pl.BlockSpec((s_pad, ATT_WIDTH), lambda i: (i, 0)),
        compiler_params=_params(1),
        name="sample_band_attention",
    )(q, cache_k, cache_v, k, v, za, bias_c, bias_n)


def _ssd_kernel(*refs, chunk, n_valid, has_init):
    if has_init:
        (xc_ref, dt_ref, dtT_ref, dtb_ref, dtbT_ref, alog_ref, alogT_ref, dsk_ref, h0_ref,
         y_ref, hout_ref, xw, hT) = refs
    else:
        (xc_ref, dt_ref, dtT_ref, dtb_ref, dtbT_ref, alog_ref, alogT_ref, dsk_ref,
         y_ref, hout_ref, xw, hT) = refs
    L = chunk
    c = pl.program_id(1)
    n_state = SSD_GROUPS * SSD_STATE

    @pl.when(c == 0)
    def _():
        if has_init:
            for g in range(SSD_GROUPS):
                hT[g] = h0_ref[g * GROUP_W:(g + 1) * GROUP_W, :].T
        else:
            hT[...] = jnp.zeros(hT.shape, F32)

    dt = _softplus(dt_ref[...] + dtb_ref[...])
    dtT = _softplus(dtT_ref[...] + dtbT_ref[...])
    if n_valid < L:
        dt = jnp.where(lax.broadcasted_iota(jnp.int32, dt.shape, 0) < n_valid, dt, 0.0)
        dtT = jnp.where(lax.broadcasted_iota(jnp.int32, dtT.shape, 1) < n_valid, dtT, 0.0)
    da = dt * (-jnp.exp(alog_ref[...]))
    daT = dtT * (-jnp.exp(alogT_ref[...]))
    ri = lax.broadcasted_iota(jnp.int32, (L, L), 0)
    ci = lax.broadcasted_iota(jnp.int32, (L, L), 1)
    causal = ri >= ci
    lower = jnp.where(causal, 1.0, 0.0).astype(BF16)
    upper = jnp.where(ri <= ci, 1.0, 0.0).astype(BF16)
    acs = sum(_dot(lower, part) for part in _split3(da))
    acsT = sum(_dot(part, upper) for part in _split3(daT))
    acs_last = acs[L - 1:L, :]
    w_end = dt * jnp.exp(acs_last - acs)
    e_last = jnp.exp(acs_last)
    col_term = acs * LOG2E
    row_term = (acsT - jnp.log(dtT)) * LOG2E

    first = lax.broadcasted_iota(jnp.int32, (1, LANES), 1) < SSD_HEAD_DIM

    for g in range(SSD_GROUPS):
        b_g = xc_ref[:, SSD_WIDTH + g * SSD_STATE:SSD_WIDTH + (g + 1) * SSD_STATE]
        c_g = xc_ref[:, SSD_WIDTH + n_state + g * SSD_STATE:SSD_WIDTH + n_state + (g + 1) * SSD_STATE]
        cb = _dot_nt(c_g, b_g)
        h_old = hT[g]
        y_state = _dot(c_g, h_old.astype(BF16))
        decays = []
        for pp in range(HEADS_PER_GROUP // 2):
            e0 = g * HEADS_PER_GROUP + 2 * pp
            col0 = g * GROUP_W + pp * LANES
            xp_b = xc_ref[:, col0:col0 + LANES]
            xp = xp_b.astype(F32)
            cols = [jnp.broadcast_to(col_term[:, e0 + a:e0 + a + 1], (L, LANES)) for a in range(2)]
            y = y_state[:, pp * LANES:(pp + 1) * LANES] * jnp.exp2(jnp.where(first, cols[0], cols[1]))
            for a in range(2):
                e = e0 + a
                seg = jnp.concatenate([cols[a]] * (L // LANES), axis=1) - row_term[e:e + 1, :]
                m = cb * jnp.exp2(jnp.where(causal, seg, NEG))
                xa = jnp.where(first if a == 0 else ~first, xp_b, jnp.zeros_like(xp_b))
                y = y + _dot(m.astype(BF16), xa)
            y_ref[:, col0:col0 + LANES] = (y + dsk_ref[:, col0:col0 + LANES] * xp).astype(y_ref.dtype)
            xw[:, pp * LANES:(pp + 1) * LANES] = (xp * _pair_cols(w_end, e0, L, first)).astype(xw.dtype)
            decays.append(_pair_cols(e_last, e0, 1, first))
        b_gt = b_g.astype(F32).T.astype(BF16)
        hT[g] = h_old * jnp.concatenate(decays, axis=1) + _dot(b_gt, xw[...])

    @pl.when(c == pl.num_programs(1) - 1)
    def _():
        for g in range(SSD_GROUPS):
            hout_ref[g * GROUP_W:(g + 1) * GROUP_W, :] = hT[g].T


def _ssd(xc, dt, dtT, wts, b, s, chunk, n_valid, h0=None):
    nc = s // chunk
    has_init = h0 is not None
    row = lambda i, c: (i * nc + c, 0)
    zero = lambda i, c: (0, 0)
    per_b = lambda i, c: (i, 0, 0)
    names = ("dt_b", "dt_bT", "a_log", "a_logT", "d_skip")
    w_list = [wts[n] for n in names]
    in_specs = [
        pl.BlockSpec((chunk, CONV_CH), row),
        pl.BlockSpec((chunk, LANES), row),
        pl.BlockSpec((LANES, chunk), lambda i, c: (0, i * nc + c)),
    ] + [_const(w.shape, zero) for w in w_list]
    args = [xc, dt, dtT] + w_list
    if has_init:
        in_specs += [pl.BlockSpec((None, SSD_HEADS * SSD_HEAD_DIM, SSD_STATE), per_b)]
        args += [h0]
    return pl.pallas_call(
        functools.partial(_ssd_kernel, chunk=chunk, n_valid=n_valid, has_init=has_init),
        out_shape=[jax.ShapeDtypeStruct((b * s, SSD_WIDTH), BF16),
                   jax.ShapeDtypeStruct((b, SSD_HEADS * SSD_HEAD_DIM, SSD_STATE), F32)],
        grid=(b, nc),
        in_specs=in_specs,
        out_specs=[pl.BlockSpec((chunk, SSD_WIDTH), row),
                   pl.BlockSpec((None, SSD_HEADS * SSD_HEAD_DIM, SSD_STATE), per_b)],
        scratch_shapes=[
            pltpu.VMEM((chunk, GROUP_W), BF16),
            pltpu.VMEM((SSD_GROUPS, SSD_STATE, GROUP_W), F32),
        ],
        compiler_params=_params(2),
        name="ssd_scan",
    )(*args)


def _out_kernel(x_ref, gate_ref, att_ref, y_ref, zs_ref, g_ref, ng_ref, wap, wsp, wout, o_ref):
    att = _dot(att_ref[...], wap[...])
    ssd = None
    for g in range(SSD_GROUPS):
        cols = slice(g * GROUP_W, (g + 1) * GROUP_W)
        yg = y_ref[:, cols].astype(F32) * zs_ref[:, cols].astype(F32)
        ms = jnp.mean(yg * yg, axis=-1, keepdims=True)
        yn = (yg * lax.rsqrt(ms + EPS) * ng_ref[:, cols]).astype(BF16)
        part = _dot(yn, wsp[cols, :])
        ssd = part if ssd is None else ssd + part
    gates = g_ref[...].astype(F32)
    merged = gates[:, :D_MODEL] * att + gates[:, D_MODEL:] * ssd
    o_ref[...] = x_ref[...] + gate_ref[...] * _dot(merged.astype(BF16), wout[...])


def _output(x, gate, att, y, zs, gates, wts, tm):
    b, s, _ = x.shape
    t = b * s
    per_b = s // tm
    row = lambda i: (i, 0)
    zero = lambda i: (0, 0)
    w_list = [wts["ssd_ng"], wts["w_att_proj"], wts["w_ssd_proj"], wts["w_out"]]
    out = pl.pallas_call(
        _out_kernel,
        out_shape=jax.ShapeDtypeStruct((t, D_MODEL), F32),
        grid=(t // tm,),
        in_specs=[
            pl.BlockSpec((tm, D_MODEL), row),
            pl.BlockSpec((None, 1, D_MODEL), lambda i: (i // per_b, 0, 0)),
            pl.BlockSpec((tm, ATT_WIDTH), row),
            pl.BlockSpec((tm, SSD_WIDTH), row),
            pl.BlockSpec((tm, SSD_WIDTH), row),
            pl.BlockSpec((tm, 2 * D_MODEL), row),
        ] + [_const(w.shape, zero) for w in w_list],
        out_specs=pl.BlockSpec((tm, D_MODEL), row),
        compiler_params=_params(1),
        name="merge_output_projection",
    )(x.reshape(t, D_MODEL), gate, att, y, zs, gates, *w_list)
    return out.reshape(b, s, D_MODEL)


def _layer_weights(w_in, q_norm_g, k_norm_g, w_att_proj, conv_w, conv_b, dt_bias, a_log, d_skip, ssd_norm_g,
                   w_ssd_proj, w_out):
    sizes = (ATT_WIDTH, ATT_WIDTH, ATT_WIDTH, ATT_WIDTH, SSD_WIDTH, CONV_CH, SSD_HEADS, 2 * D_MODEL)
    offs = [0]
    for n in sizes:
        offs.append(offs[-1] + n)
    wq, wk, wv, wza, wzs, wxbc, wdt, wg = (w_in[:, offs[i]:offs[i + 1]].astype(BF16) for i in range(len(sizes)))
    pad_h = LANES - SSD_HEADS
    head_of = jnp.arange(ATT_WIDTH) // ATT_HEAD_DIM
    sel = head_of[:, None] == jnp.arange(LANES)[None, :]
    row_vec = lambda v: jnp.pad(v.astype(F32), (0, pad_h)).reshape(1, LANES)
    return {
        "wq": wq, "wk": wk, "wv": wv, "wza": wza, "wzs": wzs, "wxbc": wxbc, "wg": wg,
        "wdt": jnp.pad(wdt, ((0, 0), (0, pad_h))),
        "wdtT": jnp.pad(wdt.T, ((0, pad_h), (0, 0))),
        "gsum": jnp.where(sel, 1.0 / ATT_HEAD_DIM, 0.0).astype(BF16),
        "qg": jnp.tile(q_norm_g.astype(F32), ATT_HEADS).reshape(1, ATT_WIDTH),
        "kg": jnp.tile(k_norm_g.astype(F32), ATT_HEADS).reshape(1, ATT_WIDTH),
        "conv_w": conv_w.astype(F32), "conv_b": conv_b.astype(F32).reshape(1, CONV_CH),
        "dt_b": row_vec(dt_bias), "dt_bT": row_vec(dt_bias).reshape(LANES, 1),
        "a_log": row_vec(a_log), "a_logT": row_vec(a_log).reshape(LANES, 1),
        "d_skip": jnp.repeat(d_skip.astype(F32), SSD_HEAD_DIM).reshape(1, SSD_WIDTH),
        "ssd_ng": ssd_norm_g.astype(F32).reshape(1, SSD_WIDTH),
        "w_att_proj": w_att_proj.astype(BF16), "w_ssd_proj": w_ssd_proj.astype(BF16), "w_out": w_out.astype(BF16),
    }


def _trunk_layer(x, mod, norm_g, wts, attn_fn, conv0, chunk, n_valid, h0=None):
    b, s, _ = x.shape
    shift, scale, gate = (mod[:, :, i * D_MODEL:(i + 1) * D_MODEL] for i in range(3))
    tm = min(IN_TM, s)
    q, k, v, za, zs, xc, gates, dt, dtT, tail = _input_projection(x, shift, scale, norm_g, conv0, wts, tm,
                                                                 min(n_valid, tm))
    att = attn_fn(q, k, v, za)
    y, h_new = _ssd(xc, dt, dtT, wts, b, s, chunk, n_valid, h0)
    out = _output(x, gate, att, y, zs, gates, wts, min(OUT_TM, s))
    return out, k.reshape(b, s, ATT_WIDTH), v.reshape(b, s, ATT_WIDTH), tail[:, HALO - (CONV_WIDTH - 1):], h_new


def kernel(x_prompt, x_sample, c_prompt, c_sample, cache_k, cache_v, state_conv, state_ssm, norm_g, w_ada, b_ada, w_in, q_norm_g, k_norm_g, rel_bias, w_att_proj, conv_w, conv_b, dt_bias, a_log, d_skip, ssd_norm_g, w_ssd_proj, w_out):
    depth = w_in.shape[0]
    bp, sp, _ = x_prompt.shape
    bs, n_new, _ = x_sample.shape
    rows = cache_k.shape[2]
    assert rows == WINDOW and n_new <= CHUNK and n_new % SUBLANES == 0 and sp % SSD_L == 0 and sp % ATT_TQ == 0
    heads = (ATT_HEADS, ATT_HEAD_DIM)
    ssm_shape = (SSD_HEADS, SSD_HEAD_DIM, SSD_STATE)
    n_c = bp + bs
    c_all = jnp.pad(jnp.concatenate([c_prompt, c_sample], axis=0), ((0, (-n_c) % SUBLANES), (0, 0)))
    xp = x_prompt
    xs = jnp.pad(x_sample, ((0, 0), (0, SAMPLE_PAD - n_new), (0, 0)))
    outs = [[] for _ in range(8)]
    for l in range(depth):
        wts = _layer_weights(w_in[l], q_norm_g[l], k_norm_g[l], w_att_proj[l], conv_w[l], conv_b[l], dt_bias[l],
                             a_log[l], d_skip[l], ssd_norm_g[l], w_ssd_proj[l], w_out[l])
        mod = _modulation(c_all, w_ada[l], b_ada[l])[:, None, :]
        bias_p, bias_c, bias_n = _bias_tables(rel_bias[l], n_new, PAST_LEN)

        attn_p = functools.partial(_prompt_attention, bias=bias_p, b=bp, s=sp)
        conv0_p = jnp.zeros((bp, HALO, CONV_CH), F32)
        xp, kp, vp, cp, hp = _trunk_layer(xp, mod[:bp], norm_g[l], wts, attn_p, conv0_p, SSD_L, SSD_L)

        attn_s = functools.partial(
            _sample_attention, cache_k=cache_k[l].reshape(bs, rows * ATT_HEADS, ATT_HEAD_DIM),
            cache_v=cache_v[l].reshape(bs, rows * ATT_HEADS, ATT_HEAD_DIM), bias_c=bias_c, bias_n=bias_n,
            nb=bs, n_new=n_new, s_pad=SAMPLE_PAD)
        conv0_s = jnp.pad(state_conv[l].astype(F32), ((0, 0), (HALO - (CONV_WIDTH - 1), 0), (0, 0)))
        h0 = state_ssm[l].reshape(bs, SSD_HEADS * SSD_HEAD_DIM, SSD_STATE)
        xs, ks, vs, cs, hs = _trunk_layer(xs, mod[bp:n_c], norm_g[l], wts, attn_s, conv0_s, SAMPLE_PAD, n_new, h0)

        keep = min(WINDOW, sp)
        outs[0].append(kp[:, sp - keep:].astype(F32).reshape(bp, keep, *heads))
        outs[1].append(vp[:, sp - keep:].astype(F32).reshape(bp, keep, *heads))
        outs[2].append(cp)
        outs[3].append(hp.reshape(bp, *ssm_shape))
        outs[4].append(ks[:, :n_new].astype(F32).reshape(bs, n_new, *heads))
        outs[5].append(vs[:, :n_new].astype(F32).reshape(bs, n_new, *heads))
        outs[6].append(cs)
        outs[7].append(hs.reshape(bs, *ssm_shape))
    return (xp, xs[:, :n_new]) + tuple(jnp.stack(o) for o in outs)
```

```python
import functools

import jax
import jax.numpy as jnp
from jax import lax
from jax.experimental import pallas as pl
from jax.experimental.pallas import tpu as pltpu

F32 = jnp.float32
BF16 = jnp.bfloat16

D_MODEL = 1024
CHUNK = 64
BAND_CHUNKS = 8
WINDOW = BAND_CHUNKS * CHUNK
ATT_HEADS = 16
ATT_HEAD_DIM = 64
ATT_WIDTH = ATT_HEADS * ATT_HEAD_DIM
MAX_REL = 256
SSD_WIDTH = 2 * D_MODEL
SSD_HEAD_DIM = 64
SSD_HEADS = SSD_WIDTH // SSD_HEAD_DIM
SSD_GROUPS = 4
SSD_STATE = 128
CONV_WIDTH = 4
CONV_CH = SSD_WIDTH + 2 * SSD_GROUPS * SSD_STATE
PAST_LEN = 4096
EPS = 1e-6
NEG = -1e30
LOG2E = 1.4426950408889634

LANES = 128
SUBLANES = 8
VMEM_LIMIT = 56 * 1024 * 1024

IN_TM = 256
CONV_BLOCKS = 6
OUT_TM = 512
ATT_TQ = 256
ATT_TK = ATT_TQ + WINDOW
SSD_L = 256
SAMPLE_PAD = 128
HEADS_PER_GROUP = SSD_HEADS // SSD_GROUPS
GROUP_W = SSD_WIDTH // SSD_GROUPS
HALO = SUBLANES


def _const(shape, index_map):
    return pl.BlockSpec(shape, index_map, pipeline_mode=pl.Buffered(1))


def _params(n_axes):
    return pltpu.CompilerParams(dimension_semantics=("arbitrary",) * n_axes, vmem_limit_bytes=VMEM_LIMIT)


def _dot(a, b):
    return jnp.dot(a, b, preferred_element_type=F32)


def _dot_nt(a, b):
    return lax.dot_general(a, b, (((1,), (1,)), ((), ())), preferred_element_type=F32)


def _split3(x):
    hi = x.astype(BF16)
    r1 = x - hi.astype(F32)
    mid = r1.astype(BF16)
    lo = (r1 - mid.astype(F32)).astype(BF16)
    return hi, mid, lo


def _sigmoid(x):
    return 0.5 * jnp.tanh(0.5 * x) + 0.5


def _silu(x):
    h = 0.5 * x
    return h * jnp.tanh(h) + h


def _softplus(x):
    return jnp.maximum(x, 0.0) + jnp.log1p(jnp.exp(-jnp.abs(x)))


def _mod_kernel(c_ref, w_ref, b_ref, o_ref):
    c = c_ref[...]
    a = (c * jax.nn.sigmoid(c)).astype(BF16)
    o_ref[...] = _dot(a, w_ref[...].astype(BF16)) + b_ref[...]


def _modulation(c_all, w_ada, b_ada):
    n = c_all.shape[0]
    return pl.pallas_call(
        _mod_kernel,
        out_shape=jax.ShapeDtypeStruct((n, 3 * D_MODEL), F32),
        grid=(3,),
        in_specs=[
            pl.BlockSpec((n, D_MODEL), lambda j: (0, 0)),
            pl.BlockSpec((D_MODEL, D_MODEL), lambda j: (0, j)),
            pl.BlockSpec((1, D_MODEL), lambda j: (0, j)),
        ],
        out_specs=pl.BlockSpec((n, D_MODEL), lambda j: (0, j)),
        compiler_params=_params(1),
        name="adaln_modulation",
    )(c_all, w_ada, b_ada.reshape(1, -1))


def _pair_cols(mat, e0, rows, first):
    a = jnp.broadcast_to(mat[:, e0:e0 + 1], (rows, LANES))
    b = jnp.broadcast_to(mat[:, e0 + 1:e0 + 2], (rows, LANES))
    return jnp.where(first, a, b)


def _inproj_kernel(x_ref, shift_ref, scale_ref, ng_ref, conv0_ref, wq, wk, wv, wza, wzs, wxbc, wg, wdt, wdtT,
                   qg, kg, cw_ref, cb_ref, q_o, k_o, v_o, za_o, zs_o, xc_o, g_o, dt_o, dtT_o, tail_o, *xpads,
                   per_b, n_valid):
    tm = x_ref.shape[0]
    cblk = CONV_CH // len(xpads)
    lag = [CONV_WIDTH - 1 - j for j in range(CONV_WIDTH)]

    @pl.when(lax.rem(pl.program_id(0), per_b) == 0)
    def _():
        for i, xpad in enumerate(xpads):
            for j in range(CONV_WIDTH - 1):
                xpad[j, HALO:HALO + lag[j], :] = conv0_ref[HALO - lag[j]:HALO, i * cblk:(i + 1) * cblk]

    x = x_ref[...]
    ms = jnp.mean(x * x, axis=-1, keepdims=True)
    h = x * lax.rsqrt(ms + EPS) * ng_ref[...]
    h = h * (1.0 + scale_ref[...]) + shift_ref[...]
    hb = h.astype(BF16)
    first = lax.broadcasted_iota(jnp.int32, (1, LANES), 1) < ATT_HEAD_DIM

    def head_norm(y, g_ref):
        parts = []
        for p in range(ATT_WIDTH // LANES):
            yp = y[:, p * LANES:(p + 1) * LANES]
            sq = yp * yp
            head_a = jnp.sum(jnp.where(first, sq, 0.0), axis=-1, keepdims=True)
            head_b = jnp.sum(jnp.where(first, 0.0, sq), axis=-1, keepdims=True)
            r_a = lax.rsqrt(head_a * (1.0 / ATT_HEAD_DIM) + EPS)
            r_b = lax.rsqrt(head_b * (1.0 / ATT_HEAD_DIM) + EPS)
            parts.append(yp * jnp.where(first, r_a, r_b))
        return jnp.concatenate(parts, axis=1) * g_ref[...]

    def conv_matmul(i):
        raw = _dot(hb, wxbc[:, i * cblk:(i + 1) * cblk])
        for j in range(CONV_WIDTH):
            xpads[i][j, HALO + lag[j]:HALO + lag[j] + tm, :] = raw

    def conv_silu(i):
        xpad, cols = xpads[i], slice(i * cblk, (i + 1) * cblk)
        acc = cb_ref[:, cols]
        for j in range(CONV_WIDTH):
            acc = acc + xpad[j, HALO:HALO + tm, :] * cw_ref[j:j + 1, cols]
        xc_o[:, cols] = _silu(acc).astype(xc_o.dtype)
        tail_o[:, cols] = xpad[CONV_WIDTH - 1, n_valid:n_valid + HALO, :]
        for j in range(CONV_WIDTH - 1):
            xpad[j, HALO:HALO + lag[j], :] = xpad[j, HALO + tm:HALO + tm + lag[j], :]

    half = SSD_WIDTH // 2
    projections = [
        (lambda: _dot(hb, wq[...]), lambda y: q_o.__setitem__(..., head_norm(y, qg).astype(q_o.dtype))),
        (lambda: _dot(hb, wk[...]), lambda y: k_o.__setitem__(..., head_norm(y, kg).astype(k_o.dtype))),
        (lambda: _dot(hb, wv[...]), lambda y: v_o.__setitem__(..., y.astype(v_o.dtype))),
        (lambda: _dot(hb, wza[...]), lambda y: za_o.__setitem__(..., _silu(y).astype(za_o.dtype))),
    ]
    for c0 in (0, half):
        cols = slice(c0, c0 + half)
        projections.append((lambda cols=cols: _dot(hb, wzs[:, cols]),
                            lambda y, cols=cols: zs_o.__setitem__((slice(None), cols), _silu(y).astype(zs_o.dtype))))
    for c0 in (0, half):
        cols = slice(c0, c0 + half)
        projections.append((lambda cols=cols: _dot(hb, wg[:, cols]),
                            lambda y, cols=cols: g_o.__setitem__((slice(None), cols), _sigmoid(y).astype(g_o.dtype))))

    n_conv = len(xpads)
    pending = {}
    for step in range(max(n_conv, len(projections)) + 1):
        if step < n_conv:
            conv_matmul(step)
        if step < len(projections):
            pending[step] = projections[step][0]()
        if 0 <= step - 1 < n_conv:
            conv_silu(step - 1)
        if 0 <= step - 1 < len(projections):
            projections[step - 1][1](pending.pop(step - 1))
    dt_o[...] = _dot(hb, wdt[...])
    dtT_o[...] = _dot_nt(wdtT[...], hb)


def _input_projection(x, shift, scale, norm_g, conv0, wts, tm, n_valid):
    b, s, _ = x.shape
    t = b * s
    per_b = s // tm
    x2 = x.reshape(t, D_MODEL)
    row = lambda i: (i, 0)
    bat = lambda i: (i // per_b, 0, 0)
    zero = lambda i: (0, 0)
    w_names = ("wq", "wk", "wv", "wza", "wzs", "wxbc", "wg", "wdt", "wdtT", "qg", "kg", "conv_w", "conv_b")
    w_list = [wts[n] for n in w_names]
    widths = (ATT_WIDTH, ATT_WIDTH, ATT_WIDTH, ATT_WIDTH, SSD_WIDTH, CONV_CH, 2 * D_MODEL)
    out_shape = [jax.ShapeDtypeStruct((t, w), BF16) for w in widths]
    out_shape += [jax.ShapeDtypeStruct((t, LANES), F32), jax.ShapeDtypeStruct((LANES, t), F32),
                  jax.ShapeDtypeStruct((b, HALO, CONV_CH), F32)]
    out_specs = [pl.BlockSpec((tm, w), row) for w in widths]
    out_specs += [pl.BlockSpec((tm, LANES), row), pl.BlockSpec((LANES, tm), lambda i: (0, i)),
                  pl.BlockSpec((None, HALO, CONV_CH), bat)]
    return pl.pallas_call(
        functools.partial(_inproj_kernel, per_b=per_b, n_valid=n_valid),
        out_shape=out_shape,
        grid=(t // tm,),
        in_specs=[
            pl.BlockSpec((tm, D_MODEL), row),
            pl.BlockSpec((None, 1, D_MODEL), bat),
            pl.BlockSpec((None, 1, D_MODEL), bat),
            _const((1, D_MODEL), zero),
            pl.BlockSpec((None, HALO, CONV_CH), bat),
        ] + [_const(w.shape, zero) for w in w_list],
        out_specs=out_specs,
        scratch_shapes=[pltpu.VMEM((CONV_WIDTH, HALO + tm + HALO, CONV_CH // CONV_BLOCKS), F32)
                        for _ in range(CONV_BLOCKS)],
        compiler_params=_params(1),
        name="input_projection",
    )(x2, shift, scale, norm_g.reshape(1, -1), conv0, *w_list)


def _bias_kernel(rb_ref, bp_ref, bsc_ref, bsn_ref, *, n_new, past_len):
    tq, tk = ATT_TQ, ATT_TK
    w = tq + tk
    n_tab = rb_ref.shape[1]
    v = lax.broadcasted_iota(jnp.int32, (n_tab, w), 1)
    d = lax.broadcasted_iota(jnp.int32, (n_tab, w), 0)
    idx = jnp.clip(tq - v + WINDOW, -MAX_REL, MAX_REL) + MAX_REL
    onehot = jnp.where(idx == d, 1.0, 0.0).astype(BF16)
    hi, mid, lo = _split3(rb_ref[...])
    base = _dot(hi, onehot) + _dot(mid, onehot) + _dot(lo, onehot)

    chunk_of = lambda pos: jnp.right_shift(pos, CHUNK.bit_length() - 1)
    qi = chunk_of(lax.broadcasted_iota(jnp.int32, (tq, tk), 0))
    kj = chunk_of(lax.broadcasted_iota(jnp.int32, (tq, tk), 1))
    band = (kj >= qi) & (kj <= qi + BAND_CHUNKS)
    sq = chunk_of(past_len + lax.broadcasted_iota(jnp.int32, (n_new, tk), 0))
    sk = chunk_of(past_len - WINDOW + lax.broadcasted_iota(jnp.int32, (n_new, tk), 1))
    sband = (sk <= sq) & (sk >= sq - BAND_CHUNKS)
    for h in range(ATT_HEADS):
        rows = jnp.broadcast_to(base[h:h + 1, :], (tq, w))
        toep = pltpu.roll(rows, 0, 1, stride=1, stride_axis=0)[:, tq:]
        bp_ref[h] = jnp.where(band, toep * LOG2E, NEG)
        srow = jnp.where(sband, toep[:n_new, :], NEG)
        bsc_ref[h] = srow[:, :WINDOW]
        bsn_ref[h] = srow[:, WINDOW:WINDOW + n_new]


def _bias_tables(rel_bias, n_new, past_len):
    n_tab = 2 * MAX_REL + 1
    pad = (-n_tab) % LANES
    rb = jnp.pad(rel_bias, ((0, 0), (0, pad)))
    return pl.pallas_call(
        functools.partial(_bias_kernel, n_new=n_new, past_len=past_len),
        out_shape=[
            jax.ShapeDtypeStruct((ATT_HEADS, ATT_TQ, ATT_TK), F32),
            jax.ShapeDtypeStruct((ATT_HEADS, n_new, WINDOW), F32),
            jax.ShapeDtypeStruct((ATT_HEADS, n_new, n_new), F32),
        ],
        compiler_params=pltpu.CompilerParams(vmem_limit_bytes=VMEM_LIMIT),
        name="relative_bias_tables",
    )(rb)


def _softmax_pv(s_blocks, v_blocks):
    m = s_blocks[0].max(axis=-1, keepdims=True)
    for s in s_blocks[1:]:
        m = jnp.maximum(m, s.max(axis=-1, keepdims=True))
    den = None
    acc = None
    for s, vb in zip(s_blocks, v_blocks):
        e = jnp.exp(s - m)
        part = e.sum(axis=-1, keepdims=True)
        den = part if den is None else den + part
        pv = _dot(e.astype(BF16), vb)
        acc = pv if acc is None else acc + pv
    return acc * (1.0 / den)


def _attn_kernel(q_ref, k0, k1, k2, v0, v1, v2, z_ref, bias_ref, o_ref):
    tq = ATT_TQ
    t = pl.program_id(1)
    first = lax.broadcasted_iota(jnp.int32, (1, LANES), 1) < ATT_HEAD_DIM
    kblocks, vblocks = (k0, k1, k2), (v0, v1, v2)
    n_blocks = len(kblocks)
    n_masked = n_blocks - 1

    def body(masked):
        def stage_scores(h):
            p, a = divmod(h, 2)
            sl = slice(p * LANES, (p + 1) * LANES)
            q2 = q_ref[:, sl].astype(F32) * (ATT_HEAD_DIM ** -0.5 * LOG2E)
            qa = jnp.where(first if a == 0 else ~first, q2, 0.0).astype(BF16)
            s_blocks = []
            for j in range(n_blocks):
                s = _dot_nt(qa, kblocks[j][:, sl]) + bias_ref[h, :, j * tq:(j + 1) * tq]
                if masked and j < n_masked:
                    s = jnp.where(t >= n_masked - j, s, NEG)
                s_blocks.append(s)
            return s_blocks

        def stage_softmax(s_blocks):
            m = s_blocks[0].max(axis=-1, keepdims=True)
            for s in s_blocks[1:]:
                m = jnp.maximum(m, s.max(axis=-1, keepdims=True))
            e_blocks = [jnp.exp2(s - m) for s in s_blocks]
            den = e_blocks[0].sum(axis=-1, keepdims=True)
            for e in e_blocks[1:]:
                den = den + e.sum(axis=-1, keepdims=True)
            return [e.astype(BF16) for e in e_blocks], 1.0 / den

        def stage_pv(h, e_blocks, inv):
            sl = slice((h // 2) * LANES, (h // 2 + 1) * LANES)
            acc = _dot(e_blocks[0], vblocks[0][:, sl])
            for j in range(1, n_blocks):
                acc = acc + _dot(e_blocks[j], vblocks[j][:, sl])
            return acc * inv

        scores, probs, outs = {}, {}, {}
        for n in range(ATT_HEADS + 2):
            if n < ATT_HEADS:
                scores[n] = stage_scores(n)
            if 0 <= n - 1 < ATT_HEADS:
                probs[n - 1] = stage_softmax(scores.pop(n - 1))
            if 0 <= n - 2 < ATT_HEADS:
                h = n - 2
                outs[h] = stage_pv(h, *probs.pop(h))
                if h % 2 == 1:
                    sl = slice((h // 2) * LANES, (h // 2 + 1) * LANES)
                    o2 = jnp.where(first, outs.pop(h - 1), outs.pop(h))
                    o_ref[:, sl] = (o2 * z_ref[:, sl].astype(F32)).astype(o_ref.dtype)

    @pl.when(t < n_masked)
    def _():
        body(True)

    @pl.when(t >= n_masked)
    def _():
        body(False)


def _prompt_attention(q, k, v, za, bias, b, s):
    tq = ATT_TQ
    nt = s // tq
    cur = lambda i, t: (i * nt + t, 0)
    prev1 = lambda i, t: (i * nt + jnp.maximum(t - 1, 0), 0)
    prev2 = lambda i, t: (i * nt + jnp.maximum(t - 2, 0), 0)
    blk = lambda m: pl.BlockSpec((tq, ATT_WIDTH), m)
    return pl.pallas_call(
        _attn_kernel,
        out_shape=jax.ShapeDtypeStruct((b * s, ATT_WIDTH), BF16),
        grid=(b, nt),
        in_specs=[blk(cur), blk(prev2), blk(prev1), blk(cur), blk(prev2), blk(prev1), blk(cur), blk(cur),
                  _const((ATT_HEADS, tq, ATT_TK), lambda i, t: (0, 0, 0))],
        out_specs=blk(cur),
        compiler_params=_params(2),
        name="prompt_band_attention",
    )(q, k, k, k, v, v, v, za, bias)


def _sample_attn_kernel(q_ref, kc_ref, vc_ref, kn_ref, vn_ref, z_ref, bc_ref, bn_ref, o_ref):
    scale = ATT_HEAD_DIM ** -0.5
    n_new = q_ref.shape[0]
    o_ref[n_new:, :] = jnp.zeros((o_ref.shape[0] - n_new, o_ref.shape[1]), o_ref.dtype)
    first = lax.broadcasted_iota(jnp.int32, (1, LANES), 1) < ATT_HEAD_DIM
    for p in range(ATT_WIDTH // LANES):
        sl = slice(p * LANES, (p + 1) * LANES)
        q2 = q_ref[:, sl]
        kb = [kc_ref[:, sl].astype(BF16), kn_ref[:, sl]]
        vb = [vc_ref[:, sl].astype(BF16), vn_ref[:, sl]]
        outs = []
        for a in range(2):
            qa = jnp.where(first if a == 0 else ~first, q2, jnp.zeros_like(q2))
            h = 2 * p + a
            s_blocks = [_dot_nt(qa, kb[0]) * scale + bc_ref[h], _dot_nt(qa, kb[1]) * scale + bn_ref[h]]
            outs.append(_softmax_pv(s_blocks, vb))
        o2 = jnp.where(first, outs[0], outs[1])
        o_ref[:n_new, sl] = (o2 * z_ref[:, sl].astype(F32)).astype(o_ref.dtype)


def _sample_attention(q, k, v, za, cache_k, cache_v, bias_c, bias_n, nb, n_new, s_pad):
    step = s_pad // n_new
    new = pl.BlockSpec((n_new, ATT_WIDTH), lambda i: (i * step, 0))
    old = pl.BlockSpec((None, WINDOW, ATT_WIDTH), lambda i: (i, 0, 0))
    return pl.pallas_call(
        _sample_attn_kernel,
        out_shape=jax.ShapeDtypeStruct((nb * s_pad, ATT_WIDTH), BF16),
        grid=(nb,),
        in_specs=[new, old, old, new, new, new,
                  _const(bias_c.shape, lambda i: (0, 0, 0)), _const(bias_n.shape, lambda i: (0, 0, 0))],
        out_specs=pl.BlockSpec((s_pad, ATT_WIDTH), lambda i: (i, 0)),
        compiler_params=_params(1),
        name="sample_band_attention",
    )(q, cache_k, cache_v, k, v, za, bias_c, bias_n)


def _ssd_kernel(*refs, chunk, n_valid, has_init):
    if has_init:
        (xc_ref, dt_ref, dtT_ref, dtb_ref, dtbT_ref, alog_ref, alogT_ref, dsk_ref, h0_ref,
         y_ref, hout_ref, xw, hT) = refs
    else:
        (xc_ref, dt_ref, dtT_ref, dtb_ref, dtbT_ref, alog_ref, alogT_ref, dsk_ref,
         y_ref, hout_ref, xw, hT) = refs
    L = chunk
    c = pl.program_id(1)
    n_state = SSD_GROUPS * SSD_STATE

    @pl.when(c == 0)
    def _():
        if has_init:
            for g in range(SSD_GROUPS):
                hT[g] = h0_ref[g * GROUP_W:(g + 1) * GROUP_W, :].T
        else:
            hT[...] = jnp.zeros(hT.shape, F32)

    dt = _softplus(dt_ref[...] + dtb_ref[...])
    dtT = _softplus(dtT_ref[...] + dtbT_ref[...])
    if n_valid < L:
        dt = jnp.where(lax.broadcasted_iota(jnp.int32, dt.shape, 0) < n_valid, dt, 0.0)
        dtT = jnp.where(lax.broadcasted_iota(jnp.int32, dtT.shape, 1) < n_valid, dtT, 0.0)
    da = dt * (-jnp.exp(alog_ref[...]))
    daT = dtT * (-jnp.exp(alogT_ref[...]))
    ri = lax.broadcasted_iota(jnp.int32, (L, L), 0)
    ci = lax.broadcasted_iota(jnp.int32, (L, L), 1)
    causal = ri >= ci
    lower = jnp.where(causal, 1.0, 0.0).astype(BF16)
    upper = jnp.where(ri <= ci, 1.0, 0.0).astype(BF16)
    acs = sum(_dot(lower, part) for part in _split3(da))
    acsT = sum(_dot(part, upper) for part in _split3(daT))
    acs_last = acs[L - 1:L, :]
    w_end = dt * jnp.exp(acs_last - acs)
    e_last = jnp.exp(acs_last)
    col_term = acs * LOG2E
    row_term = (acsT - jnp.log(dtT)) * LOG2E

    first = lax.broadcasted_iota(jnp.int32, (1, LANES), 1) < SSD_HEAD_DIM

    for g in range(SSD_GROUPS):
        b_g = xc_ref[:, SSD_WIDTH + g * SSD_STATE:SSD_WIDTH + (g + 1) * SSD_STATE]
        c_g = xc_ref[:, SSD_WIDTH + n_state + g * SSD_STATE:SSD_WIDTH + n_state + (g + 1) * SSD_STATE]
        cb = _dot_nt(c_g, b_g)
        h_old = hT[g]
        y_state = _dot(c_g, h_old.astype(BF16))
        decays = []
        for pp in range(HEADS_PER_GROUP // 2):
            e0 = g * HEADS_PER_GROUP + 2 * pp
            col0 = g * GROUP_W + pp * LANES
            xp_b = xc_ref[:, col0:col0 + LANES]
            xp = xp_b.astype(F32)
            cols = [jnp.broadcast_to(col_term[:, e0 + a:e0 + a + 1], (L, LANES)) for a in range(2)]
            y = y_state[:, pp * LANES:(pp + 1) * LANES] * jnp.exp2(jnp.where(first, cols[0], cols[1]))
            for a in range(2):
                e = e0 + a
                seg = jnp.concatenate([cols[a]] * (L // LANES), axis=1) - row_term[e:e + 1, :]
                m = cb * jnp.exp2(jnp.where(causal, seg, NEG))
                xa = jnp.where(first if a == 0 else ~first, xp_b, jnp.zeros_like(xp_b))
                y = y + _dot(m.astype(BF16), xa)
            y_ref[:, col0:col0 + LANES] = (y + dsk_ref[:, col0:col0 + LANES] * xp).astype(y_ref.dtype)
            xw[:, pp * LANES:(pp + 1) * LANES] = (xp * _pair_cols(w_end, e0, L, first)).astype(xw.dtype)
            decays.append(_pair_cols(e_last, e0, 1, first))
        b_gt = b_g.astype(F32).T.astype(BF16)
        hT[g] = h_old * jnp.concatenate(decays, axis=1) + _dot(b_gt, xw[...])

    @pl.when(c == pl.num_programs(1) - 1)
    def _():
        for g in range(SSD_GROUPS):
            hout_ref[g * GROUP_W:(g + 1) * GROUP_W, :] = hT[g].T


def _ssd(xc, dt, dtT, wts, b, s, chunk, n_valid, h0=None):
    nc = s // chunk
    has_init = h0 is not None
    row = lambda i, c: (i * nc + c, 0)
    zero = lambda i, c: (0, 0)
    per_b = lambda i, c: (i, 0, 0)
    names = ("dt_b", "dt_bT", "a_log", "a_logT", "d_skip")
    w_list = [wts[n] for n in names]
    in_specs = [
        pl.BlockSpec((chunk, CONV_CH), row),
        pl.BlockSpec((chunk, LANES), row),
        pl.BlockSpec((LANES, chunk), lambda i, c: (0, i * nc + c)),
    ] + [_const(w.shape, zero) for w in w_list]
    args = [xc, dt, dtT] + w_list
    if has_init:
        in_specs += [pl.BlockSpec((None, SSD_HEADS * SSD_HEAD_DIM, SSD_STATE), per_b)]
        args += [h0]
    return pl.pallas_call(
        functools.partial(_ssd_kernel, chunk=chunk, n_valid=n_valid, has_init=has_init),
        out_shape=[jax.ShapeDtypeStruct((b * s, SSD_WIDTH), BF16),
                   jax.ShapeDtypeStruct((b, SSD_HEADS * SSD_HEAD_DIM, SSD_STATE), F32)],
        grid=(b, nc),
        in_specs=in_specs,
        out_specs=[pl.BlockSpec((chunk, SSD_WIDTH), row),
                   pl.BlockSpec((None, SSD_HEADS * SSD_HEAD_DIM, SSD_STATE), per_b)],
        scratch_shapes=[
            pltpu.VMEM((chunk, GROUP_W), BF16),
            pltpu.VMEM((SSD_GROUPS, SSD_STATE, GROUP_W), F32),
        ],
        compiler_params=_params(2),
        name="ssd_scan",
    )(*args)


def _out_kernel(x_ref, gate_ref, att_ref, y_ref, zs_ref, g_ref, ng_ref, wap, wsp, wout, o_ref):
    att = _dot(att_ref[...], wap[...])
    ssd = None
    for g in range(SSD_GROUPS):
        cols = slice(g * GROUP_W, (g + 1) * GROUP_W)
        yg = y_ref[:, cols].astype(F32) * zs_ref[:, cols].astype(F32)
        ms = jnp.mean(yg * yg, axis=-1, keepdims=True)
        yn = (yg * lax.rsqrt(ms + EPS) * ng_ref[:, cols]).astype(BF16)
        part = _dot(yn, wsp[cols, :])
        ssd = part if ssd is None else ssd + part
    gates = g_ref[...].astype(F32)
    merged = gates[:, :D_MODEL] * att + gates[:, D_MODEL:] * ssd
    o_ref[...] = x_ref[...] + gate_ref[...] * _dot(merged.astype(BF16), wout[...])


def _output(x, gate, att, y, zs, gates, wts, tm):
    b, s, _ = x.shape
    t = b * s
    per_b = s // tm
    row = lambda i: (i, 0)
    zero = lambda i: (0, 0)
    w_list = [wts["ssd_ng"], wts["w_att_proj"], wts["w_ssd_proj"], wts["w_out"]]
    out = pl.pallas_call(
        _out_kernel,
        out_shape=jax.ShapeDtypeStruct((t, D_MODEL), F32),
        grid=(t // tm,),
        in_specs=[
            pl.BlockSpec((tm, D_MODEL), row),
            pl.BlockSpec((None, 1, D_MODEL), lambda i: (i // per_b, 0, 0)),
            pl.BlockSpec((tm, ATT_WIDTH), row),
            pl.BlockSpec((tm, SSD_WIDTH), row),
            pl.BlockSpec((tm, SSD_WIDTH), row),
            pl.BlockSpec((tm, 2 * D_MODEL), row),
        ] + [_const(w.shape, zero) for w in w_list],
        out_specs=pl.BlockSpec((tm, D_MODEL), row),
        compiler_params=_params(1),
        name="merge_output_projection",
    )(x.reshape(t, D_MODEL), gate, att, y, zs, gates, *w_list)
    return out.reshape(b, s, D_MODEL)


def _layer_weights(w_in, q_norm_g, k_norm_g, w_att_proj, conv_w, conv_b, dt_bias, a_log, d_skip, ssd_norm_g,
                   w_ssd_proj, w_out):
    sizes = (ATT_WIDTH, ATT_WIDTH, ATT_WIDTH, ATT_WIDTH, SSD_WIDTH, CONV_CH, SSD_HEADS, 2 * D_MODEL)
    offs = [0]
    for n in sizes:
        offs.append(offs[-1] + n)
    wq, wk, wv, wza, wzs, wxbc, wdt, wg = (w_in[:, offs[i]:offs[i + 1]].astype(BF16) for i in range(len(sizes)))
    pad_h = LANES - SSD_HEADS
    row_vec = lambda v: jnp.pad(v.astype(F32), (0, pad_h)).reshape(1, LANES)
    return {
        "wq": wq, "wk": wk, "wv": wv, "wza": wza, "wzs": wzs, "wxbc": wxbc, "wg": wg,
        "wdt": jnp.pad(wdt, ((0, 0), (0, pad_h))),
        "wdtT": jnp.pad(wdt.T, ((0, pad_h), (0, 0))),
        "qg": jnp.tile(q_norm_g.astype(F32), ATT_HEADS).reshape(1, ATT_WIDTH),
        "kg": jnp.tile(k_norm_g.astype(F32), ATT_HEADS).reshape(1, ATT_WIDTH),
        "conv_w": conv_w.astype(F32), "conv_b": conv_b.astype(F32).reshape(1, CONV_CH),
        "dt_b": row_vec(dt_bias), "dt_bT": row_vec(dt_bias).reshape(LANES, 1),
        "a_log": row_vec(a_log), "a_logT": row_vec(a_log).reshape(LANES, 1),
        "d_skip": jnp.repeat(d_skip.astype(F32), SSD_HEAD_DIM).reshape(1, SSD_WIDTH),
        "ssd_ng": ssd_norm_g.astype(F32).reshape(1, SSD_WIDTH),
        "w_att_proj": w_att_proj.astype(BF16), "w_ssd_proj": w_ssd_proj.astype(BF16), "w_out": w_out.astype(BF16),
    }


def _trunk_layer(x, mod, norm_g, wts, attn_fn, conv0, chunk, n_valid, h0=None):
    b, s, _ = x.shape
    shift, scale, gate = (mod[:, :, i * D_MODEL:(i + 1) * D_MODEL] for i in range(3))
    tm = min(IN_TM, s)
    q, k, v, za, zs, xc, gates, dt, dtT, tail = _input_projection(x, shift, scale, norm_g, conv0, wts, tm,
                                                                 min(n_valid, tm))
    att = attn_fn(q, k, v, za)
    y, h_new = _ssd(xc, dt, dtT, wts, b, s, chunk, n_valid, h0)
    out = _output(x, gate, att, y, zs, gates, wts, min(OUT_TM, s))
    return out, k.reshape(b, s, ATT_WIDTH), v.reshape(b, s, ATT_WIDTH), tail[:, HALO - (CONV_WIDTH - 1):], h_new


def kernel(x_prompt, x_sample, c_prompt, c_sample, cache_k, cache_v, state_conv, state_ssm, norm_g, w_ada, b_ada, w_in, q_norm_g, k_norm_g, rel_bias, w_att_proj, conv_w, conv_b, dt_bias, a_log, d_skip, ssd_norm_g, w_ssd_proj, w_out):
    depth = w_in.shape[0]
    bp, sp, _ = x_prompt.shape
    bs, n_new, _ = x_sample.shape
    rows = cache_k.shape[2]
    assert rows == WINDOW and n_new <= CHUNK and n_new % SUBLANES == 0 and sp % SSD_L == 0 and sp % ATT_TQ == 0
    heads = (ATT_HEADS, ATT_HEAD_DIM)
    ssm_shape = (SSD_HEADS, SSD_HEAD_DIM, SSD_STATE)
    n_c = bp + bs
    c_all = jnp.pad(jnp.concatenate([c_prompt, c_sample], axis=0), ((0, (-n_c) % SUBLANES), (0, 0)))
    xp = x_prompt
    xs = jnp.pad(x_sample, ((0, 0), (0, SAMPLE_PAD - n_new), (0, 0)))
    outs = [[] for _ in range(8)]
    for l in range(depth):
        wts = _layer_weights(w_in[l], q_norm_g[l], k_norm_g[l], w_att_proj[l], conv_w[l], conv_b[l], dt_bias[l],
                             a_log[l], d_skip[l], ssd_norm_g[l], w_ssd_proj[l], w_out[l])
        mod = _modulation(c_all, w_ada[l], b_ada[l])[:, None, :]
        bias_p, bias_c, bias_n = _bias_tables(rel_bias[l], n_new, PAST_LEN)

        attn_p = functools.partial(_prompt_attention, bias=bias_p, b=bp, s=sp)
        conv0_p = jnp.zeros((bp, HALO, CONV_CH), F32)
        xp, kp, vp, cp, hp = _trunk_layer(xp, mod[:bp], norm_g[l], wts, attn_p, conv0_p, SSD_L, SSD_L)

        attn_s = functools.partial(
            _sample_attention, cache_k=cache_k[l].reshape(bs, rows, ATT_WIDTH),
            cache_v=cache_v[l].reshape(bs, rows, ATT_WIDTH), bias_c=bias_c, bias_n=bias_n,
            nb=bs, n_new=n_new, s_pad=SAMPLE_PAD)
        conv0_s = jnp.pad(state_conv[l].astype(F32), ((0, 0), (HALO - (CONV_WIDTH - 1), 0), (0, 0)))
        h0 = state_ssm[l].reshape(bs, SSD_HEADS * SSD_HEAD_DIM, SSD_STATE)
        xs, ks, vs, cs, hs = _trunk_layer(xs, mod[bp:n_c], norm_g[l], wts, attn_s, conv0_s, SAMPLE_PAD, n_new, h0)

        keep = min(WINDOW, sp)
        outs[0].append(kp[:, sp - keep:].astype(F32).reshape(bp, keep, *heads))
        outs[1].append(vp[:, sp - keep:].astype(F32).reshape(bp, keep, *heads))
        outs[2].append(cp)
        outs[3].append(hp.reshape(bp, *ssm_shape))
        outs[4].append(ks[:, :n_new].astype(F32).reshape(bs, n_new, *heads))
        outs[5].append(vs[:, :n_new].astype(F32).reshape(bs, n_new, *heads))
        outs[6].append(cs)
        outs[7].append(hs.reshape(bs, *ssm_shape))
    return (xp, xs[:, :n_new]) + tuple(jnp.stack(o) for o in outs)
```

```python
import functools

import jax
import jax.numpy as jnp
from jax import lax
from jax.experimental import pallas as pl
from jax.experimental.pallas import tpu as pltpu

F32 = jnp.float32
BF16 = jnp.bfloat16

D_MODEL = 1024
CHUNK = 64
BAND_CHUNKS = 8
WINDOW = BAND_CHUNKS * CHUNK
ATT_HEADS = 16
ATT_HEAD_DIM = 64
ATT_WIDTH = ATT_HEADS * ATT_HEAD_DIM
MAX_REL = 256
SSD_WIDTH = 2 * D_MODEL
SSD_HEAD_DIM = 64
SSD_HEADS = SSD_WIDTH // SSD_HEAD_DIM
SSD_GROUPS = 4
SSD_STATE = 128
CONV_WIDTH = 4
CONV_CH = SSD_WIDTH + 2 * SSD_GROUPS * SSD_STATE
PAST_LEN = 4096
EPS = 1e-6
NEG = -1e30
LOG2E = 1.4426950408889634

LANES = 128
SUBLANES = 8
VMEM_LIMIT = 56 * 1024 * 1024

IN_TM = 256
CONV_BLOCKS = 6
OUT_TM = 512
ATT_TQ = 256
ATT_TK = ATT_TQ + WINDOW
SSD_L = 256
SAMPLE_PAD = 128
HEADS_PER_GROUP = SSD_HEADS // SSD_GROUPS
GROUP_W = SSD_WIDTH // SSD_GROUPS
HALO = SUBLANES


def _const(shape, index_map):
    return pl.BlockSpec(shape, index_map, pipeline_mode=pl.Buffered(1))


def _params(n_axes):
    return pltpu.CompilerParams(dimension_semantics=("arbitrary",) * n_axes, vmem_limit_bytes=VMEM_LIMIT)


def _dot(a, b):
    return jnp.dot(a, b, preferred_element_type=F32)


def _dot_nt(a, b):
    return lax.dot_general(a, b, (((1,), (1,)), ((), ())), preferred_element_type=F32)


def _split3(x):
    hi = x.astype(BF16)
    r1 = x - hi.astype(F32)
    mid = r1.astype(BF16)
    lo = (r1 - mid.astype(F32)).astype(BF16)
    return hi, mid, lo


def _sigmoid(x):
    return 0.5 * jnp.tanh(0.5 * x) + 0.5


def _silu(x):
    h = 0.5 * x
    return h * jnp.tanh(h) + h


def _softplus(x):
    return jnp.maximum(x, 0.0) + jnp.log1p(jnp.exp(-jnp.abs(x)))


def _mod_kernel(c_ref, w_ref, b_ref, o_ref):
    c = c_ref[...]
    a = (c * jax.nn.sigmoid(c)).astype(BF16)
    o_ref[...] = _dot(a, w_ref[...].astype(BF16)) + b_ref[...]


def _modulation(c_all, w_ada, b_ada):
    n = c_all.shape[0]
    return pl.pallas_call(
        _mod_kernel,
        out_shape=jax.ShapeDtypeStruct((n, 3 * D_MODEL), F32),
        grid=(3,),
        in_specs=[
            pl.BlockSpec((n, D_MODEL), lambda j: (0, 0)),
            pl.BlockSpec((D_MODEL, D_MODEL), lambda j: (0, j)),
            pl.BlockSpec((1, D_MODEL), lambda j: (0, j)),
        ],
        out_specs=pl.BlockSpec((n, D_MODEL), lambda j: (0, j)),
        compiler_params=_params(1),
        name="adaln_modulation",
    )(c_all, w_ada, b_ada.reshape(1, -1))


def _pair_cols(mat, e0, rows, first):
    a = jnp.broadcast_to(mat[:, e0:e0 + 1], (rows, LANES))
    b = jnp.broadcast_to(mat[:, e0 + 1:e0 + 2], (rows, LANES))
    return jnp.where(first, a, b)


def _inproj_kernel(x_ref, shift_ref, scale_ref, ng_ref, conv0_ref, wq, wk, wv, wza, wzs, wxbc, wg, wdt, wdtT,
                   qg, kg, cw_ref, cb_ref, q_o, k_o, v_o, za_o, zs_o, xc_o, g_o, dt_o, dtT_o, tail_o, *xpads,
                   per_b, n_valid):
    tm = x_ref.shape[0]
    cblk = CONV_CH // len(xpads)
    lag = [CONV_WIDTH - 1 - j for j in range(CONV_WIDTH)]

    @pl.when(lax.rem(pl.program_id(0), per_b) == 0)
    def _():
        for i, xpad in enumerate(xpads):
            for j in range(CONV_WIDTH - 1):
                xpad[j, HALO:HALO + lag[j], :] = conv0_ref[HALO - lag[j]:HALO, i * cblk:(i + 1) * cblk]

    x = x_ref[...]
    ms = jnp.mean(x * x, axis=-1, keepdims=True)
    h = x * lax.rsqrt(ms + EPS) * ng_ref[...]
    h = h * (1.0 + scale_ref[...]) + shift_ref[...]
    hb = h.astype(BF16)
    first = lax.broadcasted_iota(jnp.int32, (1, LANES), 1) < ATT_HEAD_DIM

    def head_norm(y, g_ref):
        parts = []
        for p in range(ATT_WIDTH // LANES):
            yp = y[:, p * LANES:(p + 1) * LANES]
            sq = yp * yp
            head_a = jnp.sum(jnp.where(first, sq, 0.0), axis=-1, keepdims=True)
            head_b = jnp.sum(jnp.where(first, 0.0, sq), axis=-1, keepdims=True)
            r_a = lax.rsqrt(head_a * (1.0 / ATT_HEAD_DIM) + EPS)
            r_b = lax.rsqrt(head_b * (1.0 / ATT_HEAD_DIM) + EPS)
            parts.append(yp * jnp.where(first, r_a, r_b))
        return jnp.concatenate(parts, axis=1) * g_ref[...]

    def conv_matmul(i):
        raw = _dot(hb, wxbc[:, i * cblk:(i + 1) * cblk])
        for j in range(CONV_WIDTH):
            xpads[i][j, HALO + lag[j]:HALO + lag[j] + tm, :] = raw

    def conv_silu(i):
        xpad, cols = xpads[i], slice(i * cblk, (i + 1) * cblk)
        acc = cb_ref[:, cols]
        for j in range(CONV_WIDTH):
            acc = acc + xpad[j, HALO:HALO + tm, :] * cw_ref[j:j + 1, cols]
        xc_o[:, cols] = _silu(acc).astype(xc_o.dtype)
        tail_o[:, cols] = xpad[CONV_WIDTH - 1, n_valid:n_valid + HALO, :]
        for j in range(CONV_WIDTH - 1):
            xpad[j, HALO:HALO + lag[j], :] = xpad[j, HALO + tm:HALO + tm + lag[j], :]

    half = SSD_WIDTH // 2
    projections = [
        (lambda: _dot(hb, wq[...]), lambda y: q_o.__setitem__(..., head_norm(y, qg).astype(q_o.dtype))),
        (lambda: _dot(hb, wk[...]), lambda y: k_o.__setitem__(..., head_norm(y, kg).astype(k_o.dtype))),
        (lambda: _dot(hb, wv[...]), lambda y: v_o.__setitem__(..., y.astype(v_o.dtype))),
        (lambda: _dot(hb, wza[...]), lambda y: za_o.__setitem__(..., _silu(y).astype(za_o.dtype))),
    ]
    for c0 in (0, half):
        cols = slice(c0, c0 + half)
        projections.append((lambda cols=cols: _dot(hb, wzs[:, cols]),
                            lambda y, cols=cols: zs_o.__setitem__((slice(None), cols), _silu(y).astype(zs_o.dtype))))
    for c0 in (0, half):
        cols = slice(c0, c0 + half)
        projections.append((lambda cols=cols: _dot(hb, wg[:, cols]),
                            lambda y, cols=cols: g_o.__setitem__((slice(None), cols), _sigmoid(y).astype(g_o.dtype))))

    n_conv = len(xpads)
    pending = {}
    for step in range(max(n_conv, len(projections)) + 1):
        if step < n_conv:
            conv_matmul(step)
        if step < len(projections):
            pending[step] = projections[step][0]()
        if 0 <= step - 1 < n_conv:
            conv_silu(step - 1)
        if 0 <= step - 1 < len(projections):
            projections[step - 1][1](pending.pop(step - 1))
    dt_o[...] = _dot(hb, wdt[...])
    dtT_o[...] = _dot_nt(wdtT[...], hb)


def _input_projection(x, shift, scale, norm_g, conv0, wts, tm, n_valid):
    b, s, _ = x.shape
    t = b * s
    per_b = s // tm
    x2 = x.reshape(t, D_MODEL)
    row = lambda i: (i, 0)
    bat = lambda i: (i // per_b, 0, 0)
    zero = lambda i: (0, 0)
    w_names = ("wq", "wk", "wv", "wza", "wzs", "wxbc", "wg", "wdt", "wdtT", "qg", "kg", "conv_w", "conv_b")
    w_list = [wts[n] for n in w_names]
    widths = (ATT_WIDTH, ATT_WIDTH, ATT_WIDTH, ATT_WIDTH, SSD_WIDTH, CONV_CH, 2 * D_MODEL)
    out_shape = [jax.ShapeDtypeStruct((t, w), BF16) for w in widths]
    out_shape += [jax.ShapeDtypeStruct((t, LANES), F32), jax.ShapeDtypeStruct((LANES, t), F32),
                  jax.ShapeDtypeStruct((b, HALO, CONV_CH), F32)]
    out_specs = [pl.BlockSpec((tm, w), row) for w in widths]
    out_specs += [pl.BlockSpec((tm, LANES), row), pl.BlockSpec((LANES, tm), lambda i: (0, i)),
                  pl.BlockSpec((None, HALO, CONV_CH), bat)]
    return pl.pallas_call(
        functools.partial(_inproj_kernel, per_b=per_b, n_valid=n_valid),
        out_shape=out_shape,
        grid=(t // tm,),
        in_specs=[
            pl.BlockSpec((tm, D_MODEL), row),
            pl.BlockSpec((None, 1, D_MODEL), bat),
            pl.BlockSpec((None, 1, D_MODEL), bat),
            _const((1, D_MODEL), zero),
            pl.BlockSpec((None, HALO, CONV_CH), bat),
        ] + [_const(w.shape, zero) for w in w_list],
        out_specs=out_specs,
        scratch_shapes=[pltpu.VMEM((CONV_WIDTH, HALO + tm + HALO, CONV_CH // CONV_BLOCKS), F32)
                        for _ in range(CONV_BLOCKS)],
        compiler_params=_params(1),
        name="input_projection",
    )(x2, shift, scale, norm_g.reshape(1, -1), conv0, *w_list)


def _bias_kernel(rb_ref, bp_ref, bsc_ref, bsn_ref, *, n_new, past_len):
    tq, tk = ATT_TQ, ATT_TK
    w = tq + tk
    n_tab = rb_ref.shape[1]
    v = lax.broadcasted_iota(jnp.int32, (n_tab, w), 1)
    d = lax.broadcasted_iota(jnp.int32, (n_tab, w), 0)
    idx = jnp.clip(tq - v + WINDOW, -MAX_REL, MAX_REL) + MAX_REL
    onehot = jnp.where(idx == d, 1.0, 0.0).astype(BF16)
    hi, mid, lo = _split3(rb_ref[...])
    base = _dot(hi, onehot) + _dot(mid, onehot) + _dot(lo, onehot)

    chunk_of = lambda pos: jnp.right_shift(pos, CHUNK.bit_length() - 1)
    qi = chunk_of(lax.broadcasted_iota(jnp.int32, (tq, tk), 0))
    kj = chunk_of(lax.broadcasted_iota(jnp.int32, (tq, tk), 1))
    band = (kj >= qi) & (kj <= qi + BAND_CHUNKS)
    sq = chunk_of(past_len + lax.broadcasted_iota(jnp.int32, (n_new, tk), 0))
    sk = chunk_of(past_len - WINDOW + lax.broadcasted_iota(jnp.int32, (n_new, tk), 1))
    sband = (sk <= sq) & (sk >= sq - BAND_CHUNKS)
    for h in range(ATT_HEADS):
        rows = jnp.broadcast_to(base[h:h + 1, :], (tq, w))
        toep = pltpu.roll(rows, 0, 1, stride=1, stride_axis=0)[:, tq:]
        bp_ref[h] = jnp.where(band, toep * LOG2E, NEG)
        srow = jnp.where(sband, toep[:n_new, :], NEG)
        bsc_ref[h] = srow[:, :WINDOW]
        bsn_ref[h] = srow[:, WINDOW:WINDOW + n_new]


def _bias_tables(rel_bias, n_new, past_len):
    n_tab = 2 * MAX_REL + 1
    pad = (-n_tab) % LANES
    rb = jnp.pad(rel_bias, ((0, 0), (0, pad)))
    return pl.pallas_call(
        functools.partial(_bias_kernel, n_new=n_new, past_len=past_len),
        out_shape=[
            jax.ShapeDtypeStruct((ATT_HEADS, ATT_TQ, ATT_TK), F32),
            jax.ShapeDtypeStruct((ATT_HEADS, n_new, WINDOW), F32),
            jax.ShapeDtypeStruct((ATT_HEADS, n_new, n_new), F32),
        ],
        compiler_params=pltpu.CompilerParams(vmem_limit_bytes=VMEM_LIMIT),
        name="relative_bias_tables",
    )(rb)


def _softmax_pv(s_blocks, v_blocks):
    m = s_blocks[0].max(axis=-1, keepdims=True)
    for s in s_blocks[1:]:
        m = jnp.maximum(m, s.max(axis=-1, keepdims=True))
    den = None
    acc = None
    for s, vb in zip(s_blocks, v_blocks):
        e = jnp.exp(s - m)
        part = e.sum(axis=-1, keepdims=True)
        den = part if den is None else den + part
        pv = _dot(e.astype(BF16), vb)
        acc = pv if acc is None else acc + pv
    return acc * (1.0 / den)


def _attn_kernel(q_ref, k0, k1, k2, v0, v1, v2, z_ref, bias_ref, o_ref):
    tq = ATT_TQ
    t = pl.program_id(1)
    first = lax.broadcasted_iota(jnp.int32, (1, LANES), 1) < ATT_HEAD_DIM
    kblocks, vblocks = (k0, k1, k2), (v0, v1, v2)
    n_blocks = len(kblocks)
    n_masked = n_blocks - 1

    def body(masked):
        def stage_scores(h):
            p, a = divmod(h, 2)
            sl = slice(p * LANES, (p + 1) * LANES)
            q2 = q_ref[:, sl].astype(F32) * (ATT_HEAD_DIM ** -0.5 * LOG2E)
            qa = jnp.where(first if a == 0 else ~first, q2, 0.0).astype(BF16)
            return [_dot_nt(qa, kblocks[j][:, sl]) for j in range(n_blocks)]

        def stage_softmax(h, qk_blocks):
            half = tq // 2
            per_tile = LANES // CHUNK
            n_tiles = n_blocks * tq // LANES
            e_tiles, inv = [], []
            for r in range(2):
                rows = slice(r * half, (r + 1) * half)
                c_lo, c_hi = r * half // CHUNK, (r + 1) * half // CHUNK - 1
                tiles = {}
                for tile in range(n_tiles):
                    if tile * per_tile + per_tile - 1 < c_lo or tile * per_tile > c_hi + BAND_CHUNKS:
                        continue
                    j, lt = divmod(tile, tq // LANES)
                    s = qk_blocks[j][rows, lt * LANES:(lt + 1) * LANES] \
                        + bias_ref[h, rows, tile * LANES:(tile + 1) * LANES]
                    if masked and j < n_masked:
                        s = jnp.where(t >= n_masked - j, s, NEG)
                    tiles[tile] = s
                m = functools.reduce(jnp.maximum, tiles.values()).max(axis=-1, keepdims=True)
                exps = {tile: jnp.exp2(s - m) for tile, s in tiles.items()}
                den = functools.reduce(jnp.add, exps.values()).sum(axis=-1, keepdims=True)
                e_tiles.append({tile: e.astype(BF16) for tile, e in exps.items()})
                inv.append(1.0 / den)
            zeros = jnp.zeros((half, LANES), BF16)
            e_blocks = []
            for j in range(n_blocks):
                tile0 = j * (tq // LANES)
                e_blocks.append(jnp.concatenate(
                    [jnp.concatenate([e_tiles[r].get(tile0 + lt, zeros) for lt in range(tq // LANES)], axis=1)
                     for r in range(2)], axis=0))
            return e_blocks, jnp.concatenate(inv, axis=0)

        def stage_pv(h, e_blocks, inv):
            sl = slice((h // 2) * LANES, (h // 2 + 1) * LANES)
            acc = _dot(e_blocks[0], vblocks[0][:, sl])
            for j in range(1, n_blocks):
                acc = acc + _dot(e_blocks[j], vblocks[j][:, sl])
            return acc * inv

        scores, probs, outs = {}, {}, {}
        for n in range(ATT_HEADS + 2):
            if n < ATT_HEADS:
                scores[n] = stage_scores(n)
            if 0 <= n - 1 < ATT_HEADS:
                probs[n - 1] = stage_softmax(n - 1, scores.pop(n - 1))
            if 0 <= n - 2 < ATT_HEADS:
                h = n - 2
                outs[h] = stage_pv(h, *probs.pop(h))
                if h % 2 == 1:
                    sl = slice((h // 2) * LANES, (h // 2 + 1) * LANES)
                    o2 = jnp.where(first, outs.pop(h - 1), outs.pop(h))
                    o_ref[:, sl] = (o2 * z_ref[:, sl].astype(F32)).astype(o_ref.dtype)

    @pl.when(t < n_masked)
    def _():
        body(True)

    @pl.when(t >= n_masked)
    def _():
        body(False)


def _prompt_attention(q, k, v, za, bias, b, s):
    tq = ATT_TQ
    nt = s // tq
    cur = lambda i, t: (i * nt + t, 0)
    prev1 = lambda i, t: (i * nt + jnp.maximum(t - 1, 0), 0)
    prev2 = lambda i, t: (i * nt + jnp.maximum(t - 2, 0), 0)
    blk = lambda m: pl.BlockSpec((tq, ATT_WIDTH), m)
    return pl.pallas_call(
        _attn_kernel,
        out_shape=jax.ShapeDtypeStruct((b * s, ATT_WIDTH), BF16),
        grid=(b, nt),
        in_specs=[blk(cur), blk(prev2), blk(prev1), blk(cur), blk(prev2), blk(prev1), blk(cur), blk(cur),
                  _const((ATT_HEADS, tq, ATT_TK), lambda i, t: (0, 0, 0))],
        out_specs=blk(cur),
        compiler_params=_params(2),
        name="prompt_band_attention",
    )(q, k, k, k, v, v, v, za, bias)


def _sample_attn_kernel(q_ref, kc_ref, vc_ref, kn_ref, vn_ref, z_ref, bc_ref, bn_ref, o_ref):
    scale = ATT_HEAD_DIM ** -0.5
    n_new = q_ref.shape[0]
    o_ref[n_new:, :] = jnp.zeros((o_ref.shape[0] - n_new, o_ref.shape[1]), o_ref.dtype)
    first = lax.broadcasted_iota(jnp.int32, (1, LANES), 1) < ATT_HEAD_DIM
    for p in range(ATT_WIDTH // LANES):
        sl = slice(p * LANES, (p + 1) * LANES)
        q2 = q_ref[:, sl]
        kb = [kc_ref[:, sl].astype(BF16), kn_ref[:, sl]]
        vb = [vc_ref[:, sl].astype(BF16), vn_ref[:, sl]]
        outs = []
        for a in range(2):
            qa = jnp.where(first if a == 0 else ~first, q2, jnp.zeros_like(q2))
            h = 2 * p + a
            s_blocks = [_dot_nt(qa, kb[0]) * scale + bc_ref[h], _dot_nt(qa, kb[1]) * scale + bn_ref[h]]
            outs.append(_softmax_pv(s_blocks, vb))
        o2 = jnp.where(first, outs[0], outs[1])
        o_ref[:n_new, sl] = (o2 * z_ref[:, sl].astype(F32)).astype(o_ref.dtype)


def _sample_attention(q, k, v, za, cache_k, cache_v, bias_c, bias_n, nb, n_new, s_pad):
    step = s_pad // n_new
    new = pl.BlockSpec((n_new, ATT_WIDTH), lambda i: (i * step, 0))
    old = pl.BlockSpec((None, WINDOW, ATT_WIDTH), lambda i: (i, 0, 0))
    return pl.pallas_call(
        _sample_attn_kernel,
        out_shape=jax.ShapeDtypeStruct((nb * s_pad, ATT_WIDTH), BF16),
        grid=(nb,),
        in_specs=[new, old, old, new, new, new,
                  _const(bias_c.shape, lambda i: (0, 0, 0)), _const(bias_n.shape, lambda i: (0, 0, 0))],
        out_specs=pl.BlockSpec((s_pad, ATT_WIDTH), lambda i: (i, 0)),
        compiler_params=_params(1),
        name="sample_band_attention",
    )(q, cache_k, cache_v, k, v, za, bias_c, bias_n)


def _ssd_kernel(*refs, chunk, n_valid, has_init):
    if has_init:
        (xc_ref, dt_ref, dtT_ref, dtb_ref, dtbT_ref, alog_ref, alogT_ref, dsk_ref, h0_ref,
         y_ref, hout_ref, xw, hT) = refs
    else:
        (xc_ref, dt_ref, dtT_ref, dtb_ref, dtbT_ref, alog_ref, alogT_ref, dsk_ref,
         y_ref, hout_ref, xw, hT) = refs
    L = chunk
    c = pl.program_id(1)
    n_state = SSD_GROUPS * SSD_STATE

    @pl.when(c == 0)
    def _():
        if has_init:
            for g in range(SSD_GROUPS):
                hT[g] = h0_ref[g * GROUP_W:(g + 1) * GROUP_W, :].T
        else:
            hT[...] = jnp.zeros(hT.shape, F32)

    dt = _softplus(dt_ref[...] + dtb_ref[...])
    dtT = _softplus(dtT_ref[...] + dtbT_ref[...])
    if n_valid < L:
        dt = jnp.where(lax.broadcasted_iota(jnp.int32, dt.shape, 0) < n_valid, dt, 0.0)
        dtT = jnp.where(lax.broadcasted_iota(jnp.int32, dtT.shape, 1) < n_valid, dtT, 0.0)
    da = dt * (-jnp.exp(alog_ref[...]))
    daT = dtT * (-jnp.exp(alogT_ref[...]))
    ri = lax.broadcasted_iota(jnp.int32, (L, L), 0)
    ci = lax.broadcasted_iota(jnp.int32, (L, L), 1)
    causal = ri >= ci
    lower = jnp.where(causal, 1.0, 0.0).astype(BF16)
    upper = jnp.where(ri <= ci, 1.0, 0.0).astype(BF16)
    acs = sum(_dot(lower, part) for part in _split3(da))
    acsT = sum(_dot(part, upper) for part in _split3(daT))
    acs_last = acs[L - 1:L, :]
    w_end = dt * jnp.exp(acs_last - acs)
    e_last = jnp.exp(acs_last)
    col_term = acs * LOG2E
    row_term = (acsT - jnp.log(dtT)) * LOG2E

    first = lax.broadcasted_iota(jnp.int32, (1, LANES), 1) < SSD_HEAD_DIM

    for g in range(SSD_GROUPS):
        b_g = xc_ref[:, SSD_WIDTH + g * SSD_STATE:SSD_WIDTH + (g + 1) * SSD_STATE]
        c_g = xc_ref[:, SSD_WIDTH + n_state + g * SSD_STATE:SSD_WIDTH + n_state + (g + 1) * SSD_STATE]
        cb = _dot_nt(c_g, b_g)
        h_old = hT[g]
        y_state = _dot(c_g, h_old.astype(BF16))
        decays = []
        for pp in range(HEADS_PER_GROUP // 2):
            e0 = g * HEADS_PER_GROUP + 2 * pp
            col0 = g * GROUP_W + pp * LANES
            xp_b = xc_ref[:, col0:col0 + LANES]
            xp = xp_b.astype(F32)
            cols = [jnp.broadcast_to(col_term[:, e0 + a:e0 + a + 1], (L, LANES)) for a in range(2)]
            y = y_state[:, pp * LANES:(pp + 1) * LANES] * jnp.exp2(jnp.where(first, cols[0], cols[1]))
            for a in range(2):
                e = e0 + a
                seg = jnp.concatenate([cols[a]] * (L // LANES), axis=1) - row_term[e:e + 1, :]
                m = cb * jnp.exp2(jnp.where(causal, seg, NEG))
                xa = jnp.where(first if a == 0 else ~first, xp_b, jnp.zeros_like(xp_b))
                y = y + _dot(m.astype(BF16), xa)
            y_ref[:, col0:col0 + LANES] = (y + dsk_ref[:, col0:col0 + LANES] * xp).astype(y_ref.dtype)
            xw[:, pp * LANES:(pp + 1) * LANES] = (xp * _pair_cols(w_end, e0, L, first)).astype(xw.dtype)
            decays.append(_pair_cols(e_last, e0, 1, first))
        b_gt = b_g.astype(F32).T.astype(BF16)
        hT[g] = h_old * jnp.concatenate(decays, axis=1) + _dot(b_gt, xw[...])

    @pl.when(c == pl.num_programs(1) - 1)
    def _():
        for g in range(SSD_GROUPS):
            hout_ref[g * GROUP_W:(g + 1) * GROUP_W, :] = hT[g].T


def _ssd(xc, dt, dtT, wts, b, s, chunk, n_valid, h0=None):
    nc = s // chunk
    has_init = h0 is not None
    row = lambda i, c: (i * nc + c, 0)
    zero = lambda i, c: (0, 0)
    per_b = lambda i, c: (i, 0, 0)
    names = ("dt_b", "dt_bT", "a_log", "a_logT", "d_skip")
    w_list = [wts[n] for n in names]
    in_specs = [
        pl.BlockSpec((chunk, CONV_CH), row),
        pl.BlockSpec((chunk, LANES), row),
        pl.BlockSpec((LANES, chunk), lambda i, c: (0, i * nc + c)),
    ] + [_const(w.shape, zero) for w in w_list]
    args = [xc, dt, dtT] + w_list
    if has_init:
        in_specs += [pl.BlockSpec((None, SSD_HEADS * SSD_HEAD_DIM, SSD_STATE), per_b)]
        args += [h0]
    return pl.pallas_call(
        functools.partial(_ssd_kernel, chunk=chunk, n_valid=n_valid, has_init=has_init),
        out_shape=[jax.ShapeDtypeStruct((b * s, SSD_WIDTH), BF16),
                   jax.ShapeDtypeStruct((b, SSD_HEADS * SSD_HEAD_DIM, SSD_STATE), F32)],
        grid=(b, nc),
        in_specs=in_specs,
        out_specs=[pl.BlockSpec((chunk, SSD_WIDTH), row),
                   pl.BlockSpec((None, SSD_HEADS * SSD_HEAD_DIM, SSD_STATE), per_b)],
        scratch_shapes=[
            pltpu.VMEM((chunk, GROUP_W), BF16),
            pltpu.VMEM((SSD_GROUPS, SSD_STATE, GROUP_W), F32),
        ],
        compiler_params=_params(2),
        name="ssd_scan",
    )(*args)


def _out_kernel(x_ref, gate_ref, att_ref, y_ref, zs_ref, g_ref, ng_ref, wap, wsp, wout, o_ref):
    att = _dot(att_ref[...], wap[...])
    ssd = None
    for g in range(SSD_GROUPS):
        cols = slice(g * GROUP_W, (g + 1) * GROUP_W)
        yg = y_ref[:, cols].astype(F32) * zs_ref[:, cols].astype(F32)
        ms = jnp.mean(yg * yg, axis=-1, keepdims=True)
        yn = (yg * lax.rsqrt(ms + EPS) * ng_ref[:, cols]).astype(BF16)
        part = _dot(yn, wsp[cols, :])
        ssd = part if ssd is None else ssd + part
    gates = g_ref[...].astype(F32)
    merged = gates[:, :D_MODEL] * att + gates[:, D_MODEL:] * ssd
    o_ref[...] = x_ref[...] + gate_ref[...] * _dot(merged.astype(BF16), wout[...])


def _output(x, gate, att, y, zs, gates, wts, tm):
    b, s, _ = x.shape
    t = b * s
    per_b = s // tm
    row = lambda i: (i, 0)
    zero = lambda i: (0, 0)
    w_list = [wts["ssd_ng"], wts["w_att_proj"], wts["w_ssd_proj"], wts["w_out"]]
    out = pl.pallas_call(
        _out_kernel,
        out_shape=jax.ShapeDtypeStruct((t, D_MODEL), F32),
        grid=(t // tm,),
        in_specs=[
            pl.BlockSpec((tm, D_MODEL), row),
            pl.BlockSpec((None, 1, D_MODEL), lambda i: (i // per_b, 0, 0)),
            pl.BlockSpec((tm, ATT_WIDTH), row),
            pl.BlockSpec((tm, SSD_WIDTH), row),
            pl.BlockSpec((tm, SSD_WIDTH), row),
            pl.BlockSpec((tm, 2 * D_MODEL), row),
        ] + [_const(w.shape, zero) for w in w_list],
        out_specs=pl.BlockSpec((tm, D_MODEL), row),
        compiler_params=_params(1),
        name="merge_output_projection",
    )(x.reshape(t, D_MODEL), gate, att, y, zs, gates, *w_list)
    return out.reshape(b, s, D_MODEL)


def _layer_weights(w_in, q_norm_g, k_norm_g, w_att_proj, conv_w, conv_b, dt_bias, a_log, d_skip, ssd_norm_g,
                   w_ssd_proj, w_out):
    sizes = (ATT_WIDTH, ATT_WIDTH, ATT_WIDTH, ATT_WIDTH, SSD_WIDTH, CONV_CH, SSD_HEADS, 2 * D_MODEL)
    offs = [0]
    for n in sizes:
        offs.append(offs[-1] + n)
    wq, wk, wv, wza, wzs, wxbc, wdt, wg = (w_in[:, offs[i]:offs[i + 1]].astype(BF16) for i in range(len(sizes)))
    pad_h = LANES - SSD_HEADS
    row_vec = lambda v: jnp.pad(v.astype(F32), (0, pad_h)).reshape(1, LANES)
    return {
        "wq": wq, "wk": wk, "wv": wv, "wza": wza, "wzs": wzs, "wxbc": wxbc, "wg": wg,
        "wdt": jnp.pad(wdt, ((0, 0), (0, pad_h))),
        "wdtT": jnp.pad(wdt.T, ((0, pad_h), (0, 0))),
        "qg": jnp.tile(q_norm_g.astype(F32), ATT_HEADS).reshape(1, ATT_WIDTH),
        "kg": jnp.tile(k_norm_g.astype(F32), ATT_HEADS).reshape(1, ATT_WIDTH),
        "conv_w": conv_w.astype(F32), "conv_b": conv_b.astype(F32).reshape(1, CONV_CH),
        "dt_b": row_vec(dt_bias), "dt_bT": row_vec(dt_bias).reshape(LANES, 1),
        "a_log": row_vec(a_log), "a_logT": row_vec(a_log).reshape(LANES, 1),
        "d_skip": jnp.repeat(d_skip.astype(F32), SSD_HEAD_DIM).reshape(1, SSD_WIDTH),
        "ssd_ng": ssd_norm_g.astype(F32).reshape(1, SSD_WIDTH),
        "w_att_proj": w_att_proj.astype(BF16), "w_ssd_proj": w_ssd_proj.astype(BF16), "w_out": w_out.astype(BF16),
    }


def _trunk_layer(x, mod, norm_g, wts, attn_fn, conv0, chunk, n_valid, h0=None):
    b, s, _ = x.shape
    shift, scale, gate = (mod[:, :, i * D_MODEL:(i + 1) * D_MODEL] for i in range(3))
    tm = min(IN_TM, s)
    q, k, v, za, zs, xc, gates, dt, dtT, tail = _input_projection(x, shift, scale, norm_g, conv0, wts, tm,
                                                                 min(n_valid, tm))
    att = attn_fn(q, k, v, za)
    y, h_new = _ssd(xc, dt, dtT, wts, b, s, chunk, n_valid, h0)
    out = _output(x, gate, att, y, zs, gates, wts, min(OUT_TM, s))
    return out, k.reshape(b, s, ATT_WIDTH), v.reshape(b, s, ATT_WIDTH), tail[:, HALO - (CONV_WIDTH - 1):], h_new


def kernel(x_prompt, x_sample, c_prompt, c_sample, cache_k, cache_v, state_conv, state_ssm, norm_g, w_ada, b_ada, w_in, q_norm_g, k_norm_g, rel_bias, w_att_proj, conv_w, conv_b, dt_bias, a_log, d_skip, ssd_norm_g, w_ssd_proj, w_out):
    depth = w_in.shape[0]
    bp, sp, _ = x_prompt.shape
    bs, n_new, _ = x_sample.shape
    rows = cache_k.shape[2]
    assert rows == WINDOW and n_new <= CHUNK and n_new % SUBLANES == 0 and sp % SSD_L == 0 and sp % ATT_TQ == 0
    heads = (ATT_HEADS, ATT_HEAD_DIM)
    ssm_shape = (SSD_HEADS, SSD_HEAD_DIM, SSD_STATE)
    n_c = bp + bs
    c_all = jnp.pad(jnp.concatenate([c_prompt, c_sample], axis=0), ((0, (-n_c) % SUBLANES), (0, 0)))
    xp = x_prompt
    xs = jnp.pad(x_sample, ((0, 0), (0, SAMPLE_PAD - n_new), (0, 0)))
    outs = [[] for _ in range(8)]
    for l in range(depth):
        wts = _layer_weights(w_in[l], q_norm_g[l], k_norm_g[l], w_att_proj[l], conv_w[l], conv_b[l], dt_bias[l],
                             a_log[l], d_skip[l], ssd_norm_g[l], w_ssd_proj[l], w_out[l])
        mod = _modulation(c_all, w_ada[l], b_ada[l])[:, None, :]
        bias_p, bias_c, bias_n = _bias_tables(rel_bias[l], n_new, PAST_LEN)

        attn_p = functools.partial(_prompt_attention, bias=bias_p, b=bp, s=sp)
        conv0_p = jnp.zeros((bp, HALO, CONV_CH), F32)
        xp, kp, vp, cp, hp = _trunk_layer(xp, mod[:bp], norm_g[l], wts, attn_p, conv0_p, SSD_L, SSD_L)

        attn_s = functools.partial(
            _sample_attention, cache_k=cache_k[l].reshape(bs, rows, ATT_WIDTH),
            cache_v=cache_v[l].reshape(bs, rows, ATT_WIDTH), bias_c=bias_c, bias_n=bias_n,
            nb=bs, n_new=n_new, s_pad=SAMPLE_PAD)
        conv0_s = jnp.pad(state_conv[l].astype(F32), ((0, 0), (HALO - (CONV_WIDTH - 1), 0), (0, 0)))
        h0 = state_ssm[l].reshape(bs, SSD_HEADS * SSD_HEAD_DIM, SSD_STATE)
        xs, ks, vs, cs, hs = _trunk_layer(xs, mod[bp:n_c], norm_g[l], wts, attn_s, conv0_s, SAMPLE_PAD, n_new, h0)

        keep = min(WINDOW, sp)
        outs[0].append(kp[:, sp - keep:].astype(F32).reshape(bp, keep, *heads))
        outs[1].append(vp[:, sp - keep:].astype(F32).reshape(bp, keep, *heads))
        outs[2].append(cp)
        outs[3].append(hp.reshape(bp, *ssm_shape))
        outs[4].append(ks[:, :n_new].astype(F32).reshape(bs, n_new, *heads))
        outs[5].append(vs[:, :n_new].astype(F32).reshape(bs, n_new, *heads))
        outs[6].append(cs)
        outs[7].append(hs.reshape(bs, *ssm_shape))
    return (xp, xs[:, :n_new]) + tuple(jnp.stack(o) for o in outs)
```

```python
import functools

import jax
import jax.numpy as jnp
from jax import lax
from jax.experimental import pallas as pl
from jax.experimental.pallas import tpu as pltpu

F32 = jnp.float32
BF16 = jnp.bfloat16

D_MODEL = 1024
CHUNK = 64
BAND_CHUNKS = 8
WINDOW = BAND_CHUNKS * CHUNK
ATT_HEADS = 16
ATT_HEAD_DIM = 64
ATT_WIDTH = ATT_HEADS * ATT_HEAD_DIM
MAX_REL = 256
SSD_WIDTH = 2 * D_MODEL
SSD_HEAD_DIM = 64
SSD_HEADS = SSD_WIDTH // SSD_HEAD_DIM
SSD_GROUPS = 4
SSD_STATE = 128
CONV_WIDTH = 4
CONV_CH = SSD_WIDTH + 2 * SSD_GROUPS * SSD_STATE
PAST_LEN = 4096
EPS = 1e-6
NEG = -1e30
LOG2E = 1.4426950408889634

LANES = 128
SUBLANES = 8
VMEM_LIMIT = 56 * 1024 * 1024

IN_TM = 256
CONV_BLOCKS = 4
IN_WINDOWS = ((ATT_WIDTH, 3), (ATT_WIDTH, 4), (ATT_WIDTH, 5), (ATT_WIDTH, 10), (SSD_WIDTH, 3), (CONV_CH, 0),
              (2 * D_MODEL, 4), (LANES, 88))
OUT_TM = 512
ATT_TQ = 256
ATT_TK = ATT_TQ + WINDOW
SSD_L = 256
SAMPLE_PAD = 128
HEADS_PER_GROUP = SSD_HEADS // SSD_GROUPS
GROUP_W = SSD_WIDTH // SSD_GROUPS
HALO = SUBLANES


def _const(shape, index_map):
    return pl.BlockSpec(shape, index_map, pipeline_mode=pl.Buffered(1))


def _params(n_axes):
    return pltpu.CompilerParams(dimension_semantics=("arbitrary",) * n_axes, vmem_limit_bytes=VMEM_LIMIT)


def _dot(a, b):
    return jnp.dot(a, b, preferred_element_type=F32)


def _dot_nt(a, b):
    return lax.dot_general(a, b, (((1,), (1,)), ((), ())), preferred_element_type=F32)


def _split3(x):
    hi = x.astype(BF16)
    r1 = x - hi.astype(F32)
    mid = r1.astype(BF16)
    lo = (r1 - mid.astype(F32)).astype(BF16)
    return hi, mid, lo


def _sigmoid(x):
    return 0.5 * jnp.tanh(0.5 * x) + 0.5


def _silu(x):
    h = 0.5 * x
    return h * jnp.tanh(h) + h


def _softplus(x):
    return jnp.maximum(x, 0.0) + jnp.log1p(jnp.exp(-jnp.abs(x)))


def _mod_kernel(c_ref, w_ref, b_ref, o_ref):
    c = c_ref[...]
    a = (c * jax.nn.sigmoid(c)).astype(BF16)
    o_ref[...] = _dot(a, w_ref[...].astype(BF16)) + b_ref[...]


def _modulation(c_all, w_ada, b_ada):
    n = c_all.shape[0]
    return pl.pallas_call(
        _mod_kernel,
        out_shape=jax.ShapeDtypeStruct((n, 3 * D_MODEL), F32),
        grid=(3,),
        in_specs=[
            pl.BlockSpec((n, D_MODEL), lambda j: (0, 0)),
            pl.BlockSpec((D_MODEL, D_MODEL), lambda j: (0, j)),
            pl.BlockSpec((1, D_MODEL), lambda j: (0, j)),
        ],
        out_specs=pl.BlockSpec((n, D_MODEL), lambda j: (0, j)),
        compiler_params=_params(1),
        name="adaln_modulation",
    )(c_all, w_ada, b_ada.reshape(1, -1))


def _pair_cols(mat, e0, rows, first):
    a = jnp.broadcast_to(mat[:, e0:e0 + 1], (rows, LANES))
    b = jnp.broadcast_to(mat[:, e0 + 1:e0 + 2], (rows, LANES))
    return jnp.where(first, a, b)


def _inproj_kernel(x_ref, shift_ref, scale_ref, ng_ref, conv0_ref, wq, wk, wv, wza, wzs, wxbc, wg, wdt, wdtT,
                   qg, kg, cw_ref, cb_ref, q_o, k_o, v_o, za_o, zs_o, xc_o, g_o, dt_o, dtT_o, tail_o, *xpads,
                   per_b, n_valid):
    tm = x_ref.shape[0]
    cblk = CONV_CH // len(xpads)
    lag = [CONV_WIDTH - 1 - j for j in range(CONV_WIDTH)]

    @pl.when(lax.rem(pl.program_id(0), per_b) == 0)
    def _():
        for i, xpad in enumerate(xpads):
            for j in range(CONV_WIDTH - 1):
                xpad[j, HALO:HALO + lag[j], :] = conv0_ref[HALO - lag[j]:HALO, i * cblk:(i + 1) * cblk]

    x = x_ref[...]
    ms = jnp.mean(x * x, axis=-1, keepdims=True)
    h = x * lax.rsqrt(ms + EPS) * ng_ref[...]
    h = h * (1.0 + scale_ref[...]) + shift_ref[...]
    hb = h.astype(BF16)
    first = lax.broadcasted_iota(jnp.int32, (1, LANES), 1) < ATT_HEAD_DIM

    def head_norm(y, g_ref):
        parts = []
        for p in range(ATT_WIDTH // LANES):
            yp = y[:, p * LANES:(p + 1) * LANES]
            sq = yp * yp
            head_a = jnp.sum(jnp.where(first, sq, 0.0), axis=-1, keepdims=True)
            head_b = jnp.sum(jnp.where(first, 0.0, sq), axis=-1, keepdims=True)
            r_a = lax.rsqrt(head_a * (1.0 / ATT_HEAD_DIM) + EPS)
            r_b = lax.rsqrt(head_b * (1.0 / ATT_HEAD_DIM) + EPS)
            parts.append(yp * jnp.where(first, r_a, r_b))
        return jnp.concatenate(parts, axis=1) * g_ref[...]

    def conv_matmul(i):
        raw = _dot(hb, wxbc[:, i * cblk:(i + 1) * cblk])
        for j in range(CONV_WIDTH):
            xpads[i][j, HALO + lag[j]:HALO + lag[j] + tm, :] = raw

    def conv_silu(i):
        xpad, cols = xpads[i], slice(i * cblk, (i + 1) * cblk)
        acc = cb_ref[:, cols]
        for j in range(CONV_WIDTH):
            acc = acc + xpad[j, HALO:HALO + tm, :] * cw_ref[j:j + 1, cols]
        xc_o[:, cols] = _silu(acc).astype(xc_o.dtype)
        tail_o[:, cols] = xpad[CONV_WIDTH - 1, n_valid:n_valid + HALO, :]
        for j in range(CONV_WIDTH - 1):
            xpad[j, HALO:HALO + lag[j], :] = xpad[j, HALO + tm:HALO + tm + lag[j], :]

    half = SSD_WIDTH // 2
    projections = [
        (lambda: _dot(hb, wv[...]), lambda y: v_o.__setitem__(..., y.astype(v_o.dtype))),
        (lambda: _dot(hb, wza[...]), lambda y: za_o.__setitem__(..., _silu(y).astype(za_o.dtype))),
    ]
    for c0 in (0, half):
        cols = slice(c0, c0 + half)
        projections.append((lambda cols=cols: _dot(hb, wzs[:, cols]),
                            lambda y, cols=cols: zs_o.__setitem__((slice(None), cols), _silu(y).astype(zs_o.dtype))))
    for c0 in (0, half):
        cols = slice(c0, c0 + half)
        projections.append((lambda cols=cols: _dot(hb, wg[:, cols]),
                            lambda y, cols=cols: g_o.__setitem__((slice(None), cols), _sigmoid(y).astype(g_o.dtype))))
    projections += [
        (lambda: _dot(hb, wq[...]), lambda y: q_o.__setitem__(..., head_norm(y, qg).astype(q_o.dtype))),
        (lambda: _dot(hb, wk[...]), lambda y: k_o.__setitem__(..., head_norm(y, kg).astype(k_o.dtype))),
    ]

    n_conv = len(xpads)
    pending = {}
    for step in range(max(n_conv, len(projections)) + 1):
        if step < n_conv:
            conv_matmul(step)
        if step < len(projections):
            pending[step] = projections[step][0]()
        if 0 <= step - 1 < n_conv:
            conv_silu(step - 1)
        if 0 <= step - 1 < len(projections):
            projections[step - 1][1](pending.pop(step - 1))
    dt_o[...] = _dot(hb, wdt[...])
    dtT_o[...] = _dot_nt(wdtT[...], hb)


def _input_projection(x, shift, scale, norm_g, conv0, wts, tm, n_valid):
    b, s, _ = x.shape
    t = b * s
    per_b = s // tm
    x2 = x.reshape(t, D_MODEL)
    row = lambda i: (i, 0)
    bat = lambda i: (i // per_b, 0, 0)
    zero = lambda i: (0, 0)
    w_windows = [pl.BlockSpec((D_MODEL, w), functools.partial(lambda i, j: (0, j), j=j),
                              pipeline_mode=pl.Buffered(1)) for w, j in IN_WINDOWS]
    w_list = [wts[n] for n in ("wdtT", "qg", "kg", "conv_w", "conv_b")]
    widths = (ATT_WIDTH, ATT_WIDTH, ATT_WIDTH, ATT_WIDTH, SSD_WIDTH, CONV_CH, 2 * D_MODEL)
    out_shape = [jax.ShapeDtypeStruct((t, w), BF16) for w in widths]
    out_shape += [jax.ShapeDtypeStruct((t, LANES), F32), jax.ShapeDtypeStruct((LANES, t), F32),
                  jax.ShapeDtypeStruct((b, HALO, CONV_CH), F32)]
    out_specs = [pl.BlockSpec((tm, w), row) for w in widths]
    out_specs += [pl.BlockSpec((tm, LANES), row), pl.BlockSpec((LANES, tm), lambda i: (0, i)),
                  pl.BlockSpec((None, HALO, CONV_CH), bat)]
    return pl.pallas_call(
        functools.partial(_inproj_kernel, per_b=per_b, n_valid=n_valid),
        out_shape=out_shape,
        grid=(t // tm,),
        in_specs=[
            pl.BlockSpec((tm, D_MODEL), row),
            pl.BlockSpec((None, 1, D_MODEL), bat),
            pl.BlockSpec((None, 1, D_MODEL), bat),
            _const((1, D_MODEL), zero),
            pl.BlockSpec((None, HALO, CONV_CH), bat),
        ] + w_windows + [_const(w.shape, zero) for w in w_list],
        out_specs=out_specs,
        scratch_shapes=[pltpu.VMEM((CONV_WIDTH, HALO + tm + HALO, CONV_CH // CONV_BLOCKS), F32)
                        for _ in range(CONV_BLOCKS)],
        compiler_params=_params(1),
        name="input_projection",
    )(x2, shift, scale, norm_g.reshape(1, -1), conv0, *([wts["w_all"]] * len(IN_WINDOWS)), *w_list)


def _bias_kernel(rb_ref, bp_ref, bsc_ref, bsn_ref, *, n_new, past_len):
    tq, tk = ATT_TQ, ATT_TK
    w = tq + tk
    n_tab = rb_ref.shape[1]
    v = lax.broadcasted_iota(jnp.int32, (n_tab, w), 1)
    d = lax.broadcasted_iota(jnp.int32, (n_tab, w), 0)
    idx = jnp.clip(tq - v + WINDOW, -MAX_REL, MAX_REL) + MAX_REL
    onehot = jnp.where(idx == d, 1.0, 0.0).astype(BF16)
    hi, mid, lo = _split3(rb_ref[...])
    base = _dot(hi, onehot) + _dot(mid, onehot) + _dot(lo, onehot)

    chunk_of = lambda pos: jnp.right_shift(pos, CHUNK.bit_length() - 1)
    qi = chunk_of(lax.broadcasted_iota(jnp.int32, (tq, tk), 0))
    kj = chunk_of(lax.broadcasted_iota(jnp.int32, (tq, tk), 1))
    band = (kj >= qi) & (kj <= qi + BAND_CHUNKS)
    sq = chunk_of(past_len + lax.broadcasted_iota(jnp.int32, (n_new, tk), 0))
    sk = chunk_of(past_len - WINDOW + lax.broadcasted_iota(jnp.int32, (n_new, tk), 1))
    sband = (sk <= sq) & (sk >= sq - BAND_CHUNKS)
    for h in range(ATT_HEADS):
        rows = jnp.broadcast_to(base[h:h + 1, :], (tq, w))
        toep = pltpu.roll(rows, 0, 1, stride=1, stride_axis=0)[:, tq:]
        bp_ref[h] = jnp.where(band, toep * LOG2E, NEG)
        srow = jnp.where(sband, toep[:n_new, :], NEG)
        bsc_ref[h] = srow[:, :WINDOW]
        bsn_ref[h] = srow[:, WINDOW:WINDOW + n_new]


def _bias_tables(rel_bias, n_new, past_len):
    n_tab = 2 * MAX_REL + 1
    pad = (-n_tab) % LANES
    rb = jnp.pad(rel_bias, ((0, 0), (0, pad)))
    return pl.pallas_call(
        functools.partial(_bias_kernel, n_new=n_new, past_len=past_len),
        out_shape=[
            jax.ShapeDtypeStruct((ATT_HEADS, ATT_TQ, ATT_TK), F32),
            jax.ShapeDtypeStruct((ATT_HEADS, n_new, WINDOW), F32),
            jax.ShapeDtypeStruct((ATT_HEADS, n_new, n_new), F32),
        ],
        compiler_params=pltpu.CompilerParams(vmem_limit_bytes=VMEM_LIMIT),
        name="relative_bias_tables",
    )(rb)


def _softmax_pv(s_blocks, v_blocks):
    m = s_blocks[0].max(axis=-1, keepdims=True)
    for s in s_blocks[1:]:
        m = jnp.maximum(m, s.max(axis=-1, keepdims=True))
    den = None
    acc = None
    for s, vb in zip(s_blocks, v_blocks):
        e = jnp.exp(s - m)
        part = e.sum(axis=-1, keepdims=True)
        den = part if den is None else den + part
        pv = _dot(e.astype(BF16), vb)
        acc = pv if acc is None else acc + pv
    return acc * (1.0 / den)


def _attn_kernel(q_ref, k0, k1, k2, v0, v1, v2, z_ref, bias_ref, o_ref):
    tq = ATT_TQ
    t = pl.program_id(1)
    first = lax.broadcasted_iota(jnp.int32, (1, LANES), 1) < ATT_HEAD_DIM
    kblocks, vblocks = (k0, k1, k2), (v0, v1, v2)
    n_blocks = len(kblocks)
    n_masked = n_blocks - 1

    def body(masked):
        def stage_scores(h):
            p, a = divmod(h, 2)
            sl = slice(p * LANES, (p + 1) * LANES)
            q2 = q_ref[:, sl].astype(F32) * (ATT_HEAD_DIM ** -0.5 * LOG2E)
            qa = jnp.where(first if a == 0 else ~first, q2, 0.0).astype(BF16)
            return [_dot_nt(qa, kblocks[j][:, sl]) for j in range(n_blocks)]

        def stage_softmax(h, qk_blocks):
            half = tq // 2
            per_tile = LANES // CHUNK
            n_tiles = n_blocks * tq // LANES
            e_tiles, inv = [], []
            for r in range(2):
                rows = slice(r * half, (r + 1) * half)
                c_lo, c_hi = r * half // CHUNK, (r + 1) * half // CHUNK - 1
                tiles = {}
                for tile in range(n_tiles):
                    if tile * per_tile + per_tile - 1 < c_lo or tile * per_tile > c_hi + BAND_CHUNKS:
                        continue
                    j, lt = divmod(tile, tq // LANES)
                    s = qk_blocks[j][rows, lt * LANES:(lt + 1) * LANES] \
                        + bias_ref[h, rows, tile * LANES:(tile + 1) * LANES]
                    if masked and j < n_masked:
                        s = jnp.where(t >= n_masked - j, s, NEG)
                    tiles[tile] = s
                m = functools.reduce(jnp.maximum, tiles.values()).max(axis=-1, keepdims=True)
                exps = {tile: jnp.exp2(s - m) for tile, s in tiles.items()}
                den = functools.reduce(jnp.add, exps.values()).sum(axis=-1, keepdims=True)
                e_tiles.append({tile: e.astype(BF16) for tile, e in exps.items()})
                inv.append(1.0 / den)
            zeros = jnp.zeros((half, LANES), BF16)
            e_blocks = []
            for j in range(n_blocks):
                tile0 = j * (tq // LANES)
                e_blocks.append(jnp.concatenate(
                    [jnp.concatenate([e_tiles[r].get(tile0 + lt, zeros) for lt in range(tq // LANES)], axis=1)
                     for r in range(2)], axis=0))
            return e_blocks, jnp.concatenate(inv, axis=0)

        def stage_pv(h, e_blocks, inv):
            sl = slice((h // 2) * LANES, (h // 2 + 1) * LANES)
            acc = _dot(e_blocks[0], vblocks[0][:, sl])
            for j in range(1, n_blocks):
                acc = acc + _dot(e_blocks[j], vblocks[j][:, sl])
            return acc * inv

        scores, probs, outs = {}, {}, {}
        for n in range(ATT_HEADS + 2):
            if n < ATT_HEADS:
                scores[n] = stage_scores(n)
            if 0 <= n - 1 < ATT_HEADS:
                probs[n - 1] = stage_softmax(n - 1, scores.pop(n - 1))
            if 0 <= n - 2 < ATT_HEADS:
                h = n - 2
                outs[h] = stage_pv(h, *probs.pop(h))
                if h % 2 == 1:
                    sl = slice((h // 2) * LANES, (h // 2 + 1) * LANES)
                    o2 = jnp.where(first, outs.pop(h - 1), outs.pop(h))
                    o_ref[:, sl] = (o2 * z_ref[:, sl].astype(F32)).astype(o_ref.dtype)

    @pl.when(t < n_masked)
    def _():
        body(True)

    @pl.when(t >= n_masked)
    def _():
        body(False)


def _prompt_attention(q, k, v, za, bias, b, s):
    tq = ATT_TQ
    nt = s // tq
    cur = lambda i, t: (i * nt + t, 0)
    prev1 = lambda i, t: (i * nt + jnp.maximum(t - 1, 0), 0)
    prev2 = lambda i, t: (i * nt + jnp.maximum(t - 2, 0), 0)
    blk = lambda m: pl.BlockSpec((tq, ATT_WIDTH), m)
    return pl.pallas_call(
        _attn_kernel,
        out_shape=jax.ShapeDtypeStruct((b * s, ATT_WIDTH), BF16),
        grid=(b, nt),
        in_specs=[blk(cur), blk(prev2), blk(prev1), blk(cur), blk(prev2), blk(prev1), blk(cur), blk(cur),
                  _const((ATT_HEADS, tq, ATT_TK), lambda i, t: (0, 0, 0))],
        out_specs=blk(cur),
        compiler_params=_params(2),
        name="prompt_band_attention",
    )(q, k, k, k, v, v, v, za, bias)


def _sample_attn_kernel(q_ref, kc_ref, vc_ref, kn_ref, vn_ref, z_ref, bc_ref, bn_ref, o_ref):
    scale = ATT_HEAD_DIM ** -0.5
    n_new = q_ref.shape[0]
    o_ref[n_new:, :] = jnp.zeros((o_ref.shape[0] - n_new, o_ref.shape[1]), o_ref.dtype)
    first = lax.broadcasted_iota(jnp.int32, (1, LANES), 1) < ATT_HEAD_DIM
    for p in range(ATT_WIDTH // LANES):
        sl = slice(p * LANES, (p + 1) * LANES)
        q2 = q_ref[:, sl]
        kb = [kc_ref[:, sl].astype(BF16), kn_ref[:, sl]]
        vb = [vc_ref[:, sl].astype(BF16), vn_ref[:, sl]]
        outs = []
        for a in range(2):
            qa = jnp.where(first if a == 0 else ~first, q2, jnp.zeros_like(q2))
            h = 2 * p + a
            s_blocks = [_dot_nt(qa, kb[0]) * scale + bc_ref[h], _dot_nt(qa, kb[1]) * scale + bn_ref[h]]
            outs.append(_softmax_pv(s_blocks, vb))
        o2 = jnp.where(first, outs[0], outs[1])
        o_ref[:n_new, sl] = (o2 * z_ref[:, sl].astype(F32)).astype(o_ref.dtype)


def _sample_attention(q, k, v, za, cache_k, cache_v, bias_c, bias_n, nb, n_new, s_pad):
    step = s_pad // n_new
    new = pl.BlockSpec((n_new, ATT_WIDTH), lambda i: (i * step, 0))
    old = pl.BlockSpec((None, WINDOW, ATT_WIDTH), lambda i: (i, 0, 0))
    return pl.pallas_call(
        _sample_attn_kernel,
        out_shape=jax.ShapeDtypeStruct((nb * s_pad, ATT_WIDTH), BF16),
        grid=(nb,),
        in_specs=[new, old, old, new, new, new,
                  _const(bias_c.shape, lambda i: (0, 0, 0)), _const(bias_n.shape, lambda i: (0, 0, 0))],
        out_specs=pl.BlockSpec((s_pad, ATT_WIDTH), lambda i: (i, 0)),
        compiler_params=_params(1),
        name="sample_band_attention",
    )(q, cache_k, cache_v, k, v, za, bias_c, bias_n)


def _ssd_kernel(*refs, chunk, n_valid, has_init):
    if has_init:
        (xc_ref, dt_ref, dtT_ref, dtb_ref, dtbT_ref, alog_ref, alogT_ref, dsk_ref, h0_ref,
         y_ref, hout_ref, xw, hT) = refs
    else:
        (xc_ref, dt_ref, dtT_ref, dtb_ref, dtbT_ref, alog_ref, alogT_ref, dsk_ref,
         y_ref, hout_ref, xw, hT) = refs
    L = chunk
    c = pl.program_id(1)
    n_state = SSD_GROUPS * SSD_STATE

    @pl.when(c == 0)
    def _():
        if has_init:
            for g in range(SSD_GROUPS):
                hT[g] = h0_ref[g * GROUP_W:(g + 1) * GROUP_W, :].T
        else:
            hT[...] = jnp.zeros(hT.shape, F32)

    dt = _softplus(dt_ref[...] + dtb_ref[...])
    dtT = _softplus(dtT_ref[...] + dtbT_ref[...])
    if n_valid < L:
        dt = jnp.where(lax.broadcasted_iota(jnp.int32, dt.shape, 0) < n_valid, dt, 0.0)
        dtT = jnp.where(lax.broadcasted_iota(jnp.int32, dtT.shape, 1) < n_valid, dtT, 0.0)
    da = dt * (-jnp.exp(alog_ref[...]))
    daT = dtT * (-jnp.exp(alogT_ref[...]))
    ri = lax.broadcasted_iota(jnp.int32, (L, L), 0)
    ci = lax.broadcasted_iota(jnp.int32, (L, L), 1)
    causal = ri >= ci
    lower = jnp.where(causal, 1.0, 0.0).astype(BF16)
    upper = jnp.where(ri <= ci, 1.0, 0.0).astype(BF16)
    acs = sum(_dot(lower, part) for part in _split3(da))
    acsT = sum(_dot(part, upper) for part in _split3(daT))
    acs_last = acs[L - 1:L, :]
    w_end = dt * jnp.exp(acs_last - acs)
    e_last = jnp.exp(acs_last)
    col_term = acs * LOG2E
    row_term = (acsT - jnp.log(dtT)) * LOG2E

    first = lax.broadcasted_iota(jnp.int32, (1, LANES), 1) < SSD_HEAD_DIM

    for g in range(SSD_GROUPS):
        b_g = xc_ref[:, SSD_WIDTH + g * SSD_STATE:SSD_WIDTH + (g + 1) * SSD_STATE]
        c_g = xc_ref[:, SSD_WIDTH + n_state + g * SSD_STATE:SSD_WIDTH + n_state + (g + 1) * SSD_STATE]
        cb = _dot_nt(c_g, b_g)
        h_old = hT[g]
        y_state = _dot(c_g, h_old.astype(BF16))
        decays = []
        for pp in range(HEADS_PER_GROUP // 2):
            e0 = g * HEADS_PER_GROUP + 2 * pp
            col0 = g * GROUP_W + pp * LANES
            xp_b = xc_ref[:, col0:col0 + LANES]
            xp = xp_b.astype(F32)
            cols = [jnp.broadcast_to(col_term[:, e0 + a:e0 + a + 1], (L, LANES)) for a in range(2)]
            y = y_state[:, pp * LANES:(pp + 1) * LANES] * jnp.exp2(jnp.where(first, cols[0], cols[1]))
            for a in range(2):
                e = e0 + a
                seg = jnp.concatenate([cols[a]] * (L // LANES), axis=1) - row_term[e:e + 1, :]
                m = cb * jnp.exp2(jnp.where(causal, seg, NEG))
                xa = jnp.where(first if a == 0 else ~first, xp_b, jnp.zeros_like(xp_b))
                y = y + _dot(m.astype(BF16), xa)
            y_ref[:, col0:col0 + LANES] = (y + dsk_ref[:, col0:col0 + LANES] * xp).astype(y_ref.dtype)
            xw[:, pp * LANES:(pp + 1) * LANES] = (xp * _pair_cols(w_end, e0, L, first)).astype(xw.dtype)
            decays.append(_pair_cols(e_last, e0, 1, first))
        b_gt = b_g.astype(F32).T.astype(BF16)
        hT[g] = h_old * jnp.concatenate(decays, axis=1) + _dot(b_gt, xw[...])

    @pl.when(c == pl.num_programs(1) - 1)
    def _():
        for g in range(SSD_GROUPS):
            hout_ref[g * GROUP_W:(g + 1) * GROUP_W, :] = hT[g].T


def _ssd(xc, dt, dtT, wts, b, s, chunk, n_valid, h0=None):
    nc = s // chunk
    has_init = h0 is not None
    row = lambda i, c: (i * nc + c, 0)
    zero = lambda i, c: (0, 0)
    per_b = lambda i, c: (i, 0, 0)
    names = ("dt_b", "dt_bT", "a_log", "a_logT", "d_skip")
    w_list = [wts[n] for n in names]
    in_specs = [
        pl.BlockSpec((chunk, CONV_CH), row),
        pl.BlockSpec((chunk, LANES), row),
        pl.BlockSpec((LANES, chunk), lambda i, c: (0, i * nc + c)),
    ] + [_const(w.shape, zero) for w in w_list]
    args = [xc, dt, dtT] + w_list
    if has_init:
        in_specs += [pl.BlockSpec((None, SSD_HEADS * SSD_HEAD_DIM, SSD_STATE), per_b)]
        args += [h0]
    return pl.pallas_call(
        functools.partial(_ssd_kernel, chunk=chunk, n_valid=n_valid, has_init=has_init),
        out_shape=[jax.ShapeDtypeStruct((b * s, SSD_WIDTH), BF16),
                   jax.ShapeDtypeStruct((b, SSD_HEADS * SSD_HEAD_DIM, SSD_STATE), F32)],
        grid=(b, nc),
        in_specs=in_specs,
        out_specs=[pl.BlockSpec((chunk, SSD_WIDTH), row),
                   pl.BlockSpec((None, SSD_HEADS * SSD_HEAD_DIM, SSD_STATE), per_b)],
        scratch_shapes=[
            pltpu.VMEM((chunk, GROUP_W), BF16),
            pltpu.VMEM((SSD_GROUPS, SSD_STATE, GROUP_W), F32),
        ],
        compiler_params=_params(2),
        name="ssd_scan",
    )(*args)


def _out_kernel(x_ref, gate_ref, att_ref, y_ref, zs_ref, g_ref, ng_ref, wap, wsp, wout, o_ref):
    att = _dot(att_ref[...], wap[...])
    ssd = None
    for g in range(SSD_GROUPS):
        cols = slice(g * GROUP_W, (g + 1) * GROUP_W)
        yg = y_ref[:, cols].astype(F32) * zs_ref[:, cols].astype(F32)
        ms = jnp.mean(yg * yg, axis=-1, keepdims=True)
        yn = (yg * lax.rsqrt(ms + EPS) * ng_ref[:, cols]).astype(BF16)
        part = _dot(yn, wsp[cols, :])
        ssd = part if ssd is None else ssd + part
    gates = g_ref[...].astype(F32)
    merged = gates[:, :D_MODEL] * att + gates[:, D_MODEL:] * ssd
    o_ref[...] = x_ref[...] + gate_ref[...] * _dot(merged.astype(BF16), wout[...])


def _output(x, gate, att, y, zs, gates, wts, tm):
    b, s, _ = x.shape
    t = b * s
    per_b = s // tm
    row = lambda i: (i, 0)
    zero = lambda i: (0, 0)
    w_list = [wts["ssd_ng"], wts["w_att_proj"], wts["w_ssd_proj"], wts["w_out"]]
    out = pl.pallas_call(
        _out_kernel,
        out_shape=jax.ShapeDtypeStruct((t, D_MODEL), F32),
        grid=(t // tm,),
        in_specs=[
            pl.BlockSpec((tm, D_MODEL), row),
            pl.BlockSpec((None, 1, D_MODEL), lambda i: (i // per_b, 0, 0)),
            pl.BlockSpec((tm, ATT_WIDTH), row),
            pl.BlockSpec((tm, SSD_WIDTH), row),
            pl.BlockSpec((tm, SSD_WIDTH), row),
            pl.BlockSpec((tm, 2 * D_MODEL), row),
        ] + [_const(w.shape, zero) for w in w_list],
        out_specs=pl.BlockSpec((tm, D_MODEL), row),
        compiler_params=_params(1),
        name="merge_output_projection",
    )(x.reshape(t, D_MODEL), gate, att, y, zs, gates, *w_list)
    return out.reshape(b, s, D_MODEL)


def _layer_weights(w_in, q_norm_g, k_norm_g, w_att_proj, conv_w, conv_b, dt_bias, a_log, d_skip, ssd_norm_g,
                   w_ssd_proj, w_out):
    sizes = (ATT_WIDTH, ATT_WIDTH, ATT_WIDTH, ATT_WIDTH, SSD_WIDTH, CONV_CH, SSD_HEADS, 2 * D_MODEL)
    offs = [0]
    for n in sizes:
        offs.append(offs[-1] + n)
    wq, wk, wv, wza, wzs, wxbc, wdt, wg = (w_in[:, offs[i]:offs[i + 1]] for i in range(len(sizes)))
    pad_h = LANES - SSD_HEADS
    row_vec = lambda v: jnp.pad(v.astype(F32), (0, pad_h)).reshape(1, LANES)
    w_all = jnp.concatenate([wxbc, wq, wk, wv, wzs, wg, wza, jnp.pad(wdt, ((0, 0), (0, pad_h)))], axis=1).astype(BF16)
    return {
        "w_all": w_all,
        "wdtT": jnp.pad(wdt.T.astype(BF16), ((0, pad_h), (0, 0))),
        "qg": jnp.tile(q_norm_g.astype(F32), ATT_HEADS).reshape(1, ATT_WIDTH),
        "kg": jnp.tile(k_norm_g.astype(F32), ATT_HEADS).reshape(1, ATT_WIDTH),
        "conv_w": conv_w.astype(F32), "conv_b": conv_b.astype(F32).reshape(1, CONV_CH),
        "dt_b": row_vec(dt_bias), "dt_bT": row_vec(dt_bias).reshape(LANES, 1),
        "a_log": row_vec(a_log), "a_logT": row_vec(a_log).reshape(LANES, 1),
        "d_skip": jnp.repeat(d_skip.astype(F32), SSD_HEAD_DIM).reshape(1, SSD_WIDTH),
        "ssd_ng": ssd_norm_g.astype(F32).reshape(1, SSD_WIDTH),
        "w_att_proj": w_att_proj.astype(BF16), "w_ssd_proj": w_ssd_proj.astype(BF16), "w_out": w_out.astype(BF16),
    }


def _trunk_layer(x, mod, norm_g, wts, attn_fn, conv0, chunk, n_valid, h0=None):
    b, s, _ = x.shape
    shift, scale, gate = (mod[:, :, i * D_MODEL:(i + 1) * D_MODEL] for i in range(3))
    tm = min(IN_TM, s)
    q, k, v, za, zs, xc, gates, dt, dtT, tail = _input_projection(x, shift, scale, norm_g, conv0, wts, tm,
                                                                 min(n_valid, tm))
    att = attn_fn(q, k, v, za)
    y, h_new = _ssd(xc, dt, dtT, wts, b, s, chunk, n_valid, h0)
    out = _output(x, gate, att, y, zs, gates, wts, min(OUT_TM, s))
    return out, k.reshape(b, s, ATT_WIDTH), v.reshape(b, s, ATT_WIDTH), tail[:, HALO - (CONV_WIDTH - 1):], h_new


def kernel(x_prompt, x_sample, c_prompt, c_sample, cache_k, cache_v, state_conv, state_ssm, norm_g, w_ada, b_ada, w_in, q_norm_g, k_norm_g, rel_bias, w_att_proj, conv_w, conv_b, dt_bias, a_log, d_skip, ssd_norm_g, w_ssd_proj, w_out):
    depth = w_in.shape[0]
    bp, sp, _ = x_prompt.shape
    bs, n_new, _ = x_sample.shape
    rows = cache_k.shape[2]
    assert rows == WINDOW and n_new <= CHUNK and n_new % SUBLANES == 0 and sp % SSD_L == 0 and sp % ATT_TQ == 0
    heads = (ATT_HEADS, ATT_HEAD_DIM)
    ssm_shape = (SSD_HEADS, SSD_HEAD_DIM, SSD_STATE)
    n_c = bp + bs
    c_all = jnp.pad(jnp.concatenate([c_prompt, c_sample], axis=0), ((0, (-n_c) % SUBLANES), (0, 0)))
    xp = x_prompt
    xs = jnp.pad(x_sample, ((0, 0), (0, SAMPLE_PAD - n_new), (0, 0)))
    outs = [[] for _ in range(8)]
    for l in range(depth):
        wts = _layer_weights(w_in[l], q_norm_g[l], k_norm_g[l], w_att_proj[l], conv_w[l], conv_b[l], dt_bias[l],
                             a_log[l], d_skip[l], ssd_norm_g[l], w_ssd_proj[l], w_out[l])
        mod = _modulation(c_all, w_ada[l], b_ada[l])[:, None, :]
        bias_p, bias_c, bias_n = _bias_tables(rel_bias[l], n_new, PAST_LEN)

        attn_p = functools.partial(_prompt_attention, bias=bias_p, b=bp, s=sp)
        conv0_p = jnp.zeros((bp, HALO, CONV_CH), F32)
        xp, kp, vp, cp, hp = _trunk_layer(xp, mod[:bp], norm_g[l], wts, attn_p, conv0_p, SSD_L, SSD_L)

        attn_s = functools.partial(
            _sample_attention, cache_k=cache_k[l].reshape(bs, rows, ATT_WIDTH),
            cache_v=cache_v[l].reshape(bs, rows, ATT_WIDTH), bias_c=bias_c, bias_n=bias_n,
            nb=bs, n_new=n_new, s_pad=SAMPLE_PAD)
        conv0_s = jnp.pad(state_conv[l].astype(F32), ((0, 0), (HALO - (CONV_WIDTH - 1), 0), (0, 0)))
        h0 = state_ssm[l].reshape(bs, SSD_HEADS * SSD_HEAD_DIM, SSD_STATE)
        xs, ks, vs, cs, hs = _trunk_layer(xs, mod[bp:n_c], norm_g[l], wts, attn_s, conv0_s, SAMPLE_PAD, n_new, h0)

        keep = min(WINDOW, sp)
        outs[0].append(kp[:, sp - keep:].astype(F32).reshape(bp, keep, *heads))
        outs[1].append(vp[:, sp - keep:].astype(F32).reshape(bp, keep, *heads))
        outs[2].append(cp)
        outs[3].append(hp.reshape(bp, *ssm_shape))
        outs[4].append(ks[:, :n_new].astype(F32).reshape(bs, n_new, *heads))
        outs[5].append(vs[:, :n_new].astype(F32).reshape(bs, n_new, *heads))
        outs[6].append(cs)
        outs[7].append(hs.reshape(bs, *ssm_shape))
    return (xp, xs[:, :n_new]) + tuple(jnp.stack(o) for o in outs)
```

```python
import functools

import jax
import jax.numpy as jnp
from jax import lax
from jax.experimental import pallas as pl
from jax.experimental.pallas import tpu as pltpu

F32 = jnp.float32
BF16 = jnp.bfloat16

D_MODEL = 1024
CHUNK = 64
BAND_CHUNKS = 8
WINDOW = BAND_CHUNKS * CHUNK
ATT_HEADS = 16
ATT_HEAD_DIM = 64
ATT_WIDTH = ATT_HEADS * ATT_HEAD_DIM
MAX_REL = 256
SSD_WIDTH = 2 * D_MODEL
SSD_HEAD_DIM = 64
SSD_HEADS = SSD_WIDTH // SSD_HEAD_DIM
SSD_GROUPS = 4
SSD_STATE = 128
CONV_WIDTH = 4
CONV_CH = SSD_WIDTH + 2 * SSD_GROUPS * SSD_STATE
PAST_LEN = 4096
EPS = 1e-6
NEG = -1e30
LOG2E = 1.4426950408889634

LANES = 128
SUBLANES = 8
VMEM_LIMIT = 56 * 1024 * 1024

IN_TM = 256
CONV_BLOCKS = 4
OUT_TM = 512
ATT_TQ = 256
ATT_TK = ATT_TQ + WINDOW
SSD_L = 256
SAMPLE_PAD = 128
HEADS_PER_GROUP = SSD_HEADS // SSD_GROUPS
GROUP_W = SSD_WIDTH // SSD_GROUPS
HALO = SUBLANES


def _const(shape, index_map):
    return pl.BlockSpec(shape, index_map, pipeline_mode=pl.Buffered(1))


def _params(n_axes):
    return pltpu.CompilerParams(dimension_semantics=("arbitrary",) * n_axes, vmem_limit_bytes=VMEM_LIMIT)


def _dot(a, b):
    return jnp.dot(a, b, preferred_element_type=F32)


def _dot_nt(a, b):
    return lax.dot_general(a, b, (((1,), (1,)), ((), ())), preferred_element_type=F32)


def _split3(x):
    hi = x.astype(BF16)
    r1 = x - hi.astype(F32)
    mid = r1.astype(BF16)
    lo = (r1 - mid.astype(F32)).astype(BF16)
    return hi, mid, lo


def _sigmoid(x):
    return 0.5 * jnp.tanh(0.5 * x) + 0.5


def _silu(x):
    h = 0.5 * x
    return h * jnp.tanh(h) + h


def _softplus(x):
    return jnp.maximum(x, 0.0) + jnp.log1p(jnp.exp(-jnp.abs(x)))


def _mod_kernel(c_ref, w_ref, b_ref, o_ref):
    c = c_ref[...]
    a = (c * jax.nn.sigmoid(c)).astype(BF16)
    o_ref[...] = _dot(a, w_ref[...].astype(BF16)) + b_ref[...]


def _modulation(c_all, w_ada, b_ada):
    n = c_all.shape[0]
    return pl.pallas_call(
        _mod_kernel,
        out_shape=jax.ShapeDtypeStruct((n, 3 * D_MODEL), F32),
        grid=(3,),
        in_specs=[
            pl.BlockSpec((n, D_MODEL), lambda j: (0, 0)),
            pl.BlockSpec((D_MODEL, D_MODEL), lambda j: (0, j)),
            pl.BlockSpec((1, D_MODEL), lambda j: (0, j)),
        ],
        out_specs=pl.BlockSpec((n, D_MODEL), lambda j: (0, j)),
        compiler_params=_params(1),
        name="adaln_modulation",
    )(c_all, w_ada, b_ada.reshape(1, -1))


def _pair_cols(mat, e0, rows, first):
    a = jnp.broadcast_to(mat[:, e0:e0 + 1], (rows, LANES))
    b = jnp.broadcast_to(mat[:, e0 + 1:e0 + 2], (rows, LANES))
    return jnp.where(first, a, b)


def _inproj_kernel(x_ref, shift_ref, scale_ref, ng_ref, conv0_ref, wq, wk, wv, wza, wzs, wxbc, wg, wdt, wdtT,
                   qg, kg, cw_ref, cb_ref, q_o, k_o, v_o, za_o, zs_o, xc_o, g_o, dt_o, dtT_o, tail_o, *xpads,
                   per_b, n_valid):
    tm = x_ref.shape[0]
    cblk = CONV_CH // len(xpads)
    lag = [CONV_WIDTH - 1 - j for j in range(CONV_WIDTH)]

    @pl.when(lax.rem(pl.program_id(0), per_b) == 0)
    def _():
        for i, xpad in enumerate(xpads):
            for j in range(CONV_WIDTH - 1):
                xpad[j, HALO:HALO + lag[j], :] = conv0_ref[HALO - lag[j]:HALO, i * cblk:(i + 1) * cblk]

    x = x_ref[...]
    ms = jnp.mean(x * x, axis=-1, keepdims=True)
    h = x * lax.rsqrt(ms + EPS) * ng_ref[...]
    h = h * (1.0 + scale_ref[...]) + shift_ref[...]
    hb = h.astype(BF16)
    first = lax.broadcasted_iota(jnp.int32, (1, LANES), 1) < ATT_HEAD_DIM

    def head_norm(y, g_ref):
        parts = []
        for p in range(ATT_WIDTH // LANES):
            yp = y[:, p * LANES:(p + 1) * LANES]
            sq = yp * yp
            head_a = jnp.sum(jnp.where(first, sq, 0.0), axis=-1, keepdims=True)
            head_b = jnp.sum(jnp.where(first, 0.0, sq), axis=-1, keepdims=True)
            r_a = lax.rsqrt(head_a * (1.0 / ATT_HEAD_DIM) + EPS)
            r_b = lax.rsqrt(head_b * (1.0 / ATT_HEAD_DIM) + EPS)
            parts.append(yp * jnp.where(first, r_a, r_b))
        return jnp.concatenate(parts, axis=1) * g_ref[...]

    def conv_matmul(i):
        raw = _dot(hb, wxbc[:, i * cblk:(i + 1) * cblk])
        for j in range(CONV_WIDTH):
            xpads[i][j, HALO + lag[j]:HALO + lag[j] + tm, :] = raw

    def conv_silu(i):
        xpad, cols = xpads[i], slice(i * cblk, (i + 1) * cblk)
        acc = cb_ref[:, cols]
        for j in range(CONV_WIDTH):
            acc = acc + xpad[j, HALO:HALO + tm, :] * cw_ref[j:j + 1, cols]
        xc_o[:, cols] = _silu(acc).astype(xc_o.dtype)
        tail_o[:, cols] = xpad[CONV_WIDTH - 1, n_valid:n_valid + HALO, :]
        for j in range(CONV_WIDTH - 1):
            xpad[j, HALO:HALO + lag[j], :] = xpad[j, HALO + tm:HALO + tm + lag[j], :]

    half = SSD_WIDTH // 2
    projections = [
        (lambda: _dot(hb, wv[...]), lambda y: v_o.__setitem__(..., y.astype(v_o.dtype))),
        (lambda: _dot(hb, wza[...]), lambda y: za_o.__setitem__(..., _silu(y).astype(za_o.dtype))),
    ]
    for c0 in (0, half):
        cols = slice(c0, c0 + half)
        projections.append((lambda cols=cols: _dot(hb, wzs[:, cols]),
                            lambda y, cols=cols: zs_o.__setitem__((slice(None), cols), _silu(y).astype(zs_o.dtype))))
    for c0 in (0, half):
        cols = slice(c0, c0 + half)
        projections.append((lambda cols=cols: _dot(hb, wg[:, cols]),
                            lambda y, cols=cols: g_o.__setitem__((slice(None), cols), _sigmoid(y).astype(g_o.dtype))))
    projections += [
        (lambda: _dot(hb, wq[...]), lambda y: q_o.__setitem__(..., head_norm(y, qg).astype(q_o.dtype))),
        (lambda: _dot(hb, wk[...]), lambda y: k_o.__setitem__(..., head_norm(y, kg).astype(k_o.dtype))),
    ]

    n_conv = len(xpads)
    pending = {}
    for step in range(max(n_conv, len(projections)) + 1):
        if step < n_conv:
            conv_matmul(step)
        if step < len(projections):
            pending[step] = projections[step][0]()
        if 0 <= step - 1 < n_conv:
            conv_silu(step - 1)
        if 0 <= step - 1 < len(projections):
            projections[step - 1][1](pending.pop(step - 1))
    dt_o[...] = _dot(hb, wdt[...])
    dtT_o[...] = _dot_nt(wdtT[...], hb)


def _input_projection(x, shift, scale, norm_g, conv0, wts, tm, n_valid):
    b, s, _ = x.shape
    t = b * s
    per_b = s // tm
    x2 = x.reshape(t, D_MODEL)
    row = lambda i: (i, 0)
    bat = lambda i: (i // per_b, 0, 0)
    zero = lambda i: (0, 0)
    w_names = ("wq", "wk", "wv", "wza", "wzs", "wxbc", "wg", "wdt", "wdtT", "qg", "kg", "conv_w", "conv_b")
    w_list = [wts[n] for n in w_names]
    widths = (ATT_WIDTH, ATT_WIDTH, ATT_WIDTH, ATT_WIDTH, SSD_WIDTH, CONV_CH, 2 * D_MODEL)
    out_shape = [jax.ShapeDtypeStruct((t, w), BF16) for w in widths]
    out_shape += [jax.ShapeDtypeStruct((t, LANES), F32), jax.ShapeDtypeStruct((LANES, t), F32),
                  jax.ShapeDtypeStruct((b, HALO, CONV_CH), F32)]
    out_specs = [pl.BlockSpec((tm, w), row) for w in widths]
    out_specs += [pl.BlockSpec((tm, LANES), row), pl.BlockSpec((LANES, tm), lambda i: (0, i)),
                  pl.BlockSpec((None, HALO, CONV_CH), bat)]
    return pl.pallas_call(
        functools.partial(_inproj_kernel, per_b=per_b, n_valid=n_valid),
        out_shape=out_shape,
        grid=(t // tm,),
        in_specs=[
            pl.BlockSpec((tm, D_MODEL), row),
            pl.BlockSpec((None, 1, D_MODEL), bat),
            pl.BlockSpec((None, 1, D_MODEL), bat),
            _const((1, D_MODEL), zero),
            pl.BlockSpec((None, HALO, CONV_CH), bat),
        ] + [_const(w.shape, zero) for w in w_list],
        out_specs=out_specs,
        scratch_shapes=[pltpu.VMEM((CONV_WIDTH, HALO + tm + HALO, CONV_CH // CONV_BLOCKS), F32)
                        for _ in range(CONV_BLOCKS)],
        compiler_params=_params(1),
        name="input_projection",
    )(x2, shift, scale, norm_g.reshape(1, -1), conv0, *w_list)


def _bias_kernel(rb_ref, bp_ref, bsc_ref, bsn_ref, *, n_new, past_len):
    tq, tk = ATT_TQ, ATT_TK
    w = tq + tk
    n_tab = rb_ref.shape[1]
    v = lax.broadcasted_iota(jnp.int32, (n_tab, w), 1)
    d = lax.broadcasted_iota(jnp.int32, (n_tab, w), 0)
    idx = jnp.clip(tq - v + WINDOW, -MAX_REL, MAX_REL) + MAX_REL
    onehot = jnp.where(idx == d, 1.0, 0.0).astype(BF16)
    hi, mid, lo = _split3(rb_ref[...])
    base = _dot(hi, onehot) + _dot(mid, onehot) + _dot(lo, onehot)

    chunk_of = lambda pos: jnp.right_shift(pos, CHUNK.bit_length() - 1)
    qi = chunk_of(lax.broadcasted_iota(jnp.int32, (tq, tk), 0))
    kj = chunk_of(lax.broadcasted_iota(jnp.int32, (tq, tk), 1))
    band = (kj >= qi) & (kj <= qi + BAND_CHUNKS)
    sq = chunk_of(past_len + lax.broadcasted_iota(jnp.int32, (n_new, tk), 0))
    sk = chunk_of(past_len - WINDOW + lax.broadcasted_iota(jnp.int32, (n_new, tk), 1))
    sband = (sk <= sq) & (sk >= sq - BAND_CHUNKS)
    for h in range(ATT_HEADS):
        rows = jnp.broadcast_to(base[h:h + 1, :], (tq, w))
        toep = pltpu.roll(rows, 0, 1, stride=1, stride_axis=0)[:, tq:]
        bp_ref[h] = jnp.where(band, toep * LOG2E, NEG)
        srow = jnp.where(sband, toep[:n_new, :], NEG)
        bsc_ref[h] = srow[:, :WINDOW]
        bsn_ref[h] = srow[:, WINDOW:WINDOW + n_new]


def _bias_tables(rel_bias, n_new, past_len):
    n_tab = 2 * MAX_REL + 1
    pad = (-n_tab) % LANES
    rb = jnp.pad(rel_bias, ((0, 0), (0, pad)))
    return pl.pallas_call(
        functools.partial(_bias_kernel, n_new=n_new, past_len=past_len),
        out_shape=[
            jax.ShapeDtypeStruct((ATT_HEADS, ATT_TQ, ATT_TK), F32),
            jax.ShapeDtypeStruct((ATT_HEADS, n_new, WINDOW), F32),
            jax.ShapeDtypeStruct((ATT_HEADS, n_new, n_new), F32),
        ],
        compiler_params=pltpu.CompilerParams(vmem_limit_bytes=VMEM_LIMIT),
        name="relative_bias_tables",
    )(rb)


def _softmax_pv(s_blocks, v_blocks):
    m = s_blocks[0].max(axis=-1, keepdims=True)
    for s in s_blocks[1:]:
        m = jnp.maximum(m, s.max(axis=-1, keepdims=True))
    den = None
    acc = None
    for s, vb in zip(s_blocks, v_blocks):
        e = jnp.exp(s - m)
        part = e.sum(axis=-1, keepdims=True)
        den = part if den is None else den + part
        pv = _dot(e.astype(BF16), vb)
        acc = pv if acc is None else acc + pv
    return acc * (1.0 / den)


def _attn_kernel(q_ref, k0, k1, k2, v0, v1, v2, z_ref, bias_ref, o_ref):
    tq = ATT_TQ
    t = pl.program_id(1)
    first = lax.broadcasted_iota(jnp.int32, (1, LANES), 1) < ATT_HEAD_DIM
    kblocks, vblocks = (k0, k1, k2), (v0, v1, v2)
    n_blocks = len(kblocks)
    n_masked = n_blocks - 1

    def body(masked):
        def stage_scores(h):
            p, a = divmod(h, 2)
            sl = slice(p * LANES, (p + 1) * LANES)
            q2 = q_ref[:, sl].astype(F32) * (ATT_HEAD_DIM ** -0.5 * LOG2E)
            qa = jnp.where(first if a == 0 else ~first, q2, 0.0).astype(BF16)
            return [_dot_nt(qa, kblocks[j][:, sl]) for j in range(n_blocks)]

        def stage_softmax(h, qk_blocks):
            half = tq // 2
            per_tile = LANES // CHUNK
            n_tiles = n_blocks * tq // LANES
            e_tiles, inv = [], []
            for r in range(2):
                rows = slice(r * half, (r + 1) * half)
                c_lo, c_hi = r * half // CHUNK, (r + 1) * half // CHUNK - 1
                tiles = {}
                for tile in range(n_tiles):
                    if tile * per_tile + per_tile - 1 < c_lo or tile * per_tile > c_hi + BAND_CHUNKS:
                        continue
                    j, lt = divmod(tile, tq // LANES)
                    s = qk_blocks[j][rows, lt * LANES:(lt + 1) * LANES] \
                        + bias_ref[h, rows, tile * LANES:(tile + 1) * LANES]
                    if masked and j < n_masked:
                        s = jnp.where(t >= n_masked - j, s, NEG)
                    tiles[tile] = s
                m = functools.reduce(jnp.maximum, tiles.values()).max(axis=-1, keepdims=True)
                exps = {tile: jnp.exp2(s - m) for tile, s in tiles.items()}
                den = functools.reduce(jnp.add, exps.values()).sum(axis=-1, keepdims=True)
                e_tiles.append({tile: e.astype(BF16) for tile, e in exps.items()})
                inv.append(1.0 / den)
            zeros = jnp.zeros((half, LANES), BF16)
            e_blocks = []
            for j in range(n_blocks):
                tile0 = j * (tq // LANES)
                e_blocks.append(jnp.concatenate(
                    [jnp.concatenate([e_tiles[r].get(tile0 + lt, zeros) for lt in range(tq // LANES)], axis=1)
                     for r in range(2)], axis=0))
            return e_blocks, jnp.concatenate(inv, axis=0)

        def stage_pv(h, e_blocks, inv):
            sl = slice((h // 2) * LANES, (h // 2 + 1) * LANES)
            acc = _dot(e_blocks[0], vblocks[0][:, sl])
            for j in range(1, n_blocks):
                acc = acc + _dot(e_blocks[j], vblocks[j][:, sl])
            return acc * inv

        scores, probs, outs = {}, {}, {}
        for n in range(ATT_HEADS + 2):
            if 0 <= n - 2 < ATT_HEADS:
                h = n - 2
                outs[h] = stage_pv(h, *probs.pop(h))
                if h % 2 == 1:
                    sl = slice((h // 2) * LANES, (h // 2 + 1) * LANES)
                    o2 = jnp.where(first, outs.pop(h - 1), outs.pop(h))
                    o_ref[:, sl] = (o2 * z_ref[:, sl].astype(F32)).astype(o_ref.dtype)
            if n < ATT_HEADS:
                scores[n] = stage_scores(n)
            if 0 <= n - 1 < ATT_HEADS:
                probs[n - 1] = stage_softmax(n - 1, scores.pop(n - 1))

    @pl.when(t < n_masked)
    def _():
        body(True)

    @pl.when(t >= n_masked)
    def _():
        body(False)


def _prompt_attention(q, k, v, za, bias, b, s):
    tq = ATT_TQ
    nt = s // tq
    cur = lambda i, t: (i * nt + t, 0)
    prev1 = lambda i, t: (i * nt + jnp.maximum(t - 1, 0), 0)
    prev2 = lambda i, t: (i * nt + jnp.maximum(t - 2, 0), 0)
    blk = lambda m: pl.BlockSpec((tq, ATT_WIDTH), m)
    return pl.pallas_call(
        _attn_kernel,
        out_shape=jax.ShapeDtypeStruct((b * s, ATT_WIDTH), BF16),
        grid=(b, nt),
        in_specs=[blk(cur), blk(prev2), blk(prev1), blk(cur), blk(prev2), blk(prev1), blk(cur), blk(cur),
                  _const((ATT_HEADS, tq, ATT_TK), lambda i, t: (0, 0, 0))],
        out_specs=blk(cur),
        compiler_params=_params(2),
        name="prompt_band_attention",
    )(q, k, k, k, v, v, v, za, bias)


def _sample_attn_kernel(q_ref, kc_ref, vc_ref, kn_ref, vn_ref, z_ref, bc_ref, bn_ref, o_ref):
    scale = ATT_HEAD_DIM ** -0.5
    n_new = q_ref.shape[0]
    o_ref[n_new:, :] = jnp.zeros((o_ref.shape[0] - n_new, o_ref.shape[1]), o_ref.dtype)
    first = lax.broadcasted_iota(jnp.int32, (1, LANES), 1) < ATT_HEAD_DIM
    for p in range(ATT_WIDTH // LANES):
        sl = slice(p * LANES, (p + 1) * LANES)
        q2 = q_ref[:, sl]
        kb = [kc_ref[:, sl].astype(BF16), kn_ref[:, sl]]
        vb = [vc_ref[:, sl].astype(BF16), vn_ref[:, sl]]
        outs = []
        for a in range(2):
            qa = jnp.where(first if a == 0 else ~first, q2, jnp.zeros_like(q2))
            h = 2 * p + a
            s_blocks = [_dot_nt(qa, kb[0]) * scale + bc_ref[h], _dot_nt(qa, kb[1]) * scale + bn_ref[h]]
            outs.append(_softmax_pv(s_blocks, vb))
        o2 = jnp.where(first, outs[0], outs[1])
        o_ref[:n_new, sl] = (o2 * z_ref[:, sl].astype(F32)).astype(o_ref.dtype)


def _sample_attention(q, k, v, za, cache_k, cache_v, bias_c, bias_n, nb, n_new, s_pad):
    step = s_pad // n_new
    new = pl.BlockSpec((n_new, ATT_WIDTH), lambda i: (i * step, 0))
    old = pl.BlockSpec((None, WINDOW, ATT_WIDTH), lambda i: (i, 0, 0))
    return pl.pallas_call(
        _sample_attn_kernel,
        out_shape=jax.ShapeDtypeStruct((nb * s_pad, ATT_WIDTH), BF16),
        grid=(nb,),
        in_specs=[new, old, old, new, new, new,
                  _const(bias_c.shape, lambda i: (0, 0, 0)), _const(bias_n.shape, lambda i: (0, 0, 0))],
        out_specs=pl.BlockSpec((s_pad, ATT_WIDTH), lambda i: (i, 0)),
        compiler_params=_params(1),
        name="sample_band_attention",
    )(q, cache_k, cache_v, k, v, za, bias_c, bias_n)


def _ssd_kernel(*refs, chunk, n_valid, has_init):
    if has_init:
        (xc_ref, dt_ref, dtT_ref, dtb_ref, dtbT_ref, alog_ref, alogT_ref, dsk_ref, h0_ref,
         y_ref, hout_ref, xw, hT) = refs
    else:
        (xc_ref, dt_ref, dtT_ref, dtb_ref, dtbT_ref, alog_ref, alogT_ref, dsk_ref,
         y_ref, hout_ref, xw, hT) = refs
    L = chunk
    c = pl.program_id(1)
    n_state = SSD_GROUPS * SSD_STATE

    @pl.when(c == 0)
    def _():
        if has_init:
            for g in range(SSD_GROUPS):
                hT[g] = h0_ref[g * GROUP_W:(g + 1) * GROUP_W, :].T
        else:
            hT[...] = jnp.zeros(hT.shape, F32)

    dt = _softplus(dt_ref[...] + dtb_ref[...])
    dtT = _softplus(dtT_ref[...] + dtbT_ref[...])
    if n_valid < L:
        dt = jnp.where(lax.broadcasted_iota(jnp.int32, dt.shape, 0) < n_valid, dt, 0.0)
        dtT = jnp.where(lax.broadcasted_iota(jnp.int32, dtT.shape, 1) < n_valid, dtT, 0.0)
    da = dt * (-jnp.exp(alog_ref[...]))
    daT = dtT * (-jnp.exp(alogT_ref[...]))
    ri = lax.broadcasted_iota(jnp.int32, (L, L), 0)
    ci = lax.broadcasted_iota(jnp.int32, (L, L), 1)
    causal = ri >= ci
    lower = jnp.where(causal, 1.0, 0.0).astype(BF16)
    upper = jnp.where(ri <= ci, 1.0, 0.0).astype(BF16)
    acs = sum(_dot(lower, part) for part in _split3(da))
    acsT = sum(_dot(part, upper) for part in _split3(daT))
    acs_last = acs[L - 1:L, :]
    w_end = dt * jnp.exp(acs_last - acs)
    e_last = jnp.exp(acs_last)
    col_term = acs * LOG2E
    row_term = (acsT - jnp.log(dtT)) * LOG2E

    first = lax.broadcasted_iota(jnp.int32, (1, LANES), 1) < SSD_HEAD_DIM

    for g in range(SSD_GROUPS):
        b_g = xc_ref[:, SSD_WIDTH + g * SSD_STATE:SSD_WIDTH + (g + 1) * SSD_STATE]
        c_g = xc_ref[:, SSD_WIDTH + n_state + g * SSD_STATE:SSD_WIDTH + n_state + (g + 1) * SSD_STATE]
        cb = _dot_nt(c_g, b_g)
        h_old = hT[g]
        y_state = _dot(c_g, h_old.astype(BF16))
        decays = []
        for pp in range(HEADS_PER_GROUP // 2):
            e0 = g * HEADS_PER_GROUP + 2 * pp
            col0 = g * GROUP_W + pp * LANES
            xp_b = xc_ref[:, col0:col0 + LANES]
            xp = xp_b.astype(F32)
            cols = [jnp.broadcast_to(col_term[:, e0 + a:e0 + a + 1], (L, LANES)) for a in range(2)]
            y = y_state[:, pp * LANES:(pp + 1) * LANES] * jnp.exp2(jnp.where(first, cols[0], cols[1]))
            for a in range(2):
                e = e0 + a
                seg = jnp.concatenate([cols[a]] * (L // LANES), axis=1) - row_term[e:e + 1, :]
                m = cb * jnp.exp2(jnp.where(causal, seg, NEG))
                xa = jnp.where(first if a == 0 else ~first, xp_b, jnp.zeros_like(xp_b))
                y = y + _dot(m.astype(BF16), xa)
            y_ref[:, col0:col0 + LANES] = (y + dsk_ref[:, col0:col0 + LANES] * xp).astype(y_ref.dtype)
            xw[:, pp * LANES:(pp + 1) * LANES] = (xp * _pair_cols(w_end, e0, L, first)).astype(xw.dtype)
            decays.append(_pair_cols(e_last, e0, 1, first))
        b_gt = b_g.astype(F32).T.astype(BF16)
        hT[g] = h_old * jnp.concatenate(decays, axis=1) + _dot(b_gt, xw[...])

    @pl.when(c == pl.num_programs(1) - 1)
    def _():
        for g in range(SSD_GROUPS):
            hout_ref[g * GROUP_W:(g + 1) * GROUP_W, :] = hT[g].T


def _ssd(xc, dt, dtT, wts, b, s, chunk, n_valid, h0=None):
    nc = s // chunk
    has_init = h0 is not None
    row = lambda i, c: (i * nc + c, 0)
    zero = lambda i, c: (0, 0)
    per_b = lambda i, c: (i, 0, 0)
    names = ("dt_b", "dt_bT", "a_log", "a_logT", "d_skip")
    w_list = [wts[n] for n in names]
    in_specs = [
        pl.BlockSpec((chunk, CONV_CH), row),
        pl.BlockSpec((chunk, LANES), row),
        pl.BlockSpec((LANES, chunk), lambda i, c: (0, i * nc + c)),
    ] + [_const(w.shape, zero) for w in w_list]
    args = [xc, dt, dtT] + w_list
    if has_init:
        in_specs += [pl.BlockSpec((None, SSD_HEADS * SSD_HEAD_DIM, SSD_STATE), per_b)]
        args += [h0]
    return pl.pallas_call(
        functools.partial(_ssd_kernel, chunk=chunk, n_valid=n_valid, has_init=has_init),
        out_shape=[jax.ShapeDtypeStruct((b * s, SSD_WIDTH), BF16),
                   jax.ShapeDtypeStruct((b, SSD_HEADS * SSD_HEAD_DIM, SSD_STATE), F32)],
        grid=(b, nc),
        in_specs=in_specs,
        out_specs=[pl.BlockSpec((chunk, SSD_WIDTH), row),
                   pl.BlockSpec((None, SSD_HEADS * SSD_HEAD_DIM, SSD_STATE), per_b)],
        scratch_shapes=[
            pltpu.VMEM((chunk, GROUP_W), BF16),
            pltpu.VMEM((SSD_GROUPS, SSD_STATE, GROUP_W), F32),
        ],
        compiler_params=_params(2),
        name="ssd_scan",
    )(*args)


def _out_kernel(x_ref, gate_ref, att_ref, y_ref, zs_ref, g_ref, ng_ref, wap, wsp, wout, o_ref):
    att = _dot(att_ref[...], wap[...])
    ssd = None
    for g in range(SSD_GROUPS):
        cols = slice(g * GROUP_W, (g + 1) * GROUP_W)
        yg = y_ref[:, cols].astype(F32) * zs_ref[:, cols].astype(F32)
        ms = jnp.mean(yg * yg, axis=-1, keepdims=True)
        yn = (yg * lax.rsqrt(ms + EPS) * ng_ref[:, cols]).astype(BF16)
        part = _dot(yn, wsp[cols, :])
        ssd = part if ssd is None else ssd + part
    gates = g_ref[...].astype(F32)
    merged = gates[:, :D_MODEL] * att + gates[:, D_MODEL:] * ssd
    o_ref[...] = x_ref[...] + gate_ref[...] * _dot(merged.astype(BF16), wout[...])


def _output(x, gate, att, y, zs, gates, wts, tm):
    b, s, _ = x.shape
    t = b * s
    per_b = s // tm
    row = lambda i: (i, 0)
    zero = lambda i: (0, 0)
    w_list = [wts["ssd_ng"], wts["w_att_proj"], wts["w_ssd_proj"], wts["w_out"]]
    out = pl.pallas_call(
        _out_kernel,
        out_shape=jax.ShapeDtypeStruct((t, D_MODEL), F32),
        grid=(t // tm,),
        in_specs=[
            pl.BlockSpec((tm, D_MODEL), row),
            pl.BlockSpec((None, 1, D_MODEL), lambda i: (i // per_b, 0, 0)),
            pl.BlockSpec((tm, ATT_WIDTH), row),
            pl.BlockSpec((tm, SSD_WIDTH), row),
            pl.BlockSpec((tm, SSD_WIDTH), row),
            pl.BlockSpec((tm, 2 * D_MODEL), row),
        ] + [_const(w.shape, zero) for w in w_list],
        out_specs=pl.BlockSpec((tm, D_MODEL), row),
        compiler_params=_params(1),
        name="merge_output_projection",
    )(x.reshape(t, D_MODEL), gate, att, y, zs, gates, *w_list)
    return out.reshape(b, s, D_MODEL)


def _layer_weights(w_in, q_norm_g, k_norm_g, w_att_proj, conv_w, conv_b, dt_bias, a_log, d_skip, ssd_norm_g,
                   w_ssd_proj, w_out):
    sizes = (ATT_WIDTH, ATT_WIDTH, ATT_WIDTH, ATT_WIDTH, SSD_WIDTH, CONV_CH, SSD_HEADS, 2 * D_MODEL)
    offs = [0]
    for n in sizes:
        offs.append(offs[-1] + n)
    wq, wk, wv, wza, wzs, wxbc, wdt, wg = (w_in[:, offs[i]:offs[i + 1]].astype(BF16) for i in range(len(sizes)))
    pad_h = LANES - SSD_HEADS
    row_vec = lambda v: jnp.pad(v.astype(F32), (0, pad_h)).reshape(1, LANES)
    return {
        "wq": wq, "wk": wk, "wv": wv, "wza": wza, "wzs": wzs, "wxbc": wxbc, "wg": wg,
        "wdt": jnp.pad(wdt, ((0, 0), (0, pad_h))),
        "wdtT": jnp.pad(wdt.T, ((0, pad_h), (0, 0))),
        "qg": jnp.tile(q_norm_g.astype(F32), ATT_HEADS).reshape(1, ATT_WIDTH),
        "kg": jnp.tile(k_norm_g.astype(F32), ATT_HEADS).reshape(1, ATT_WIDTH),
        "conv_w": conv_w.astype(F32), "conv_b": conv_b.astype(F32).reshape(1, CONV_CH),
        "dt_b": row_vec(dt_bias), "dt_bT": row_vec(dt_bias).reshape(LANES, 1),
        "a_log": row_vec(a_log), "a_logT": row_vec(a_log).reshape(LANES, 1),
        "d_skip": jnp.repeat(d_skip.astype(F32), SSD_HEAD_DIM).reshape(1, SSD_WIDTH),
        "ssd_ng": ssd_norm_g.astype(F32).reshape(1, SSD_WIDTH),
        "w_att_proj": w_att_proj.astype(BF16), "w_ssd_proj": w_ssd_proj.astype(BF16), "w_out": w_out.astype(BF16),
    }


def _trunk_layer(x, mod, norm_g, wts, attn_fn, conv0, chunk, n_valid, h0=None):
    b, s, _ = x.shape
    shift, scale, gate = (mod[:, :, i * D_MODEL:(i + 1) * D_MODEL] for i in range(3))
    tm = min(IN_TM, s)
    q, k, v, za, zs, xc, gates, dt, dtT, tail = _input_projection(x, shift, scale, norm_g, conv0, wts, tm,
                                                                 min(n_valid, tm))
    att = attn_fn(q, k, v, za)
    y, h_new = _ssd(xc, dt, dtT, wts, b, s, chunk, n_valid, h0)
    out = _output(x, gate, att, y, zs, gates, wts, min(OUT_TM, s))
    return out, k.reshape(b, s, ATT_WIDTH), v.reshape(b, s, ATT_WIDTH), tail[:, HALO - (CONV_WIDTH - 1):], h_new


def kernel(x_prompt, x_sample, c_prompt, c_sample, cache_k, cache_v, state_conv, state_ssm, norm_g, w_ada, b_ada, w_in, q_norm_g, k_norm_g, rel_bias, w_att_proj, conv_w, conv_b, dt_bias, a_log, d_skip, ssd_norm_g, w_ssd_proj, w_out):
    depth = w_in.shape[0]
    bp, sp, _ = x_prompt.shape
    bs, n_new, _ = x_sample.shape
    rows = cache_k.shape[2]
    assert rows == WINDOW and n_new <= CHUNK and n_new % SUBLANES == 0 and sp % SSD_L == 0 and sp % ATT_TQ == 0
    heads = (ATT_HEADS, ATT_HEAD_DIM)
    ssm_shape = (SSD_HEADS, SSD_HEAD_DIM, SSD_STATE)
    n_c = bp + bs
    c_all = jnp.pad(jnp.concatenate([c_prompt, c_sample], axis=0), ((0, (-n_c) % SUBLANES), (0, 0)))
    xp = x_prompt
    xs = jnp.pad(x_sample, ((0, 0), (0, SAMPLE_PAD - n_new), (0, 0)))
    outs = [[] for _ in range(8)]
    for l in range(depth):
        wts = _layer_weights(w_in[l], q_norm_g[l], k_norm_g[l], w_att_proj[l], conv_w[l], conv_b[l], dt_bias[l],
                             a_log[l], d_skip[l], ssd_norm_g[l], w_ssd_proj[l], w_out[l])
        mod = _modulation(c_all, w_ada[l], b_ada[l])[:, None, :]
        bias_p, bias_c, bias_n = _bias_tables(rel_bias[l], n_new, PAST_LEN)

        attn_p = functools.partial(_prompt_attention, bias=bias_p, b=bp, s=sp)
        conv0_p = jnp.zeros((bp, HALO, CONV_CH), F32)
        xp, kp, vp, cp, hp = _trunk_layer(xp, mod[:bp], norm_g[l], wts, attn_p, conv0_p, SSD_L, SSD_L)

        attn_s = functools.partial(
            _sample_attention, cache_k=cache_k[l].reshape(bs, rows, ATT_WIDTH),
            cache_v=cache_v[l].reshape(bs, rows, ATT_WIDTH), bias_c=bias_c, bias_n=bias_n,
            nb=bs, n_new=n_new, s_pad=SAMPLE_PAD)
        conv0_s = jnp.pad(state_conv[l].astype(F32), ((0, 0), (HALO - (CONV_WIDTH - 1), 0), (0, 0)))
        h0 = state_ssm[l].reshape(bs, SSD_HEADS * SSD_HEAD_DIM, SSD_STATE)
        xs, ks, vs, cs, hs = _trunk_layer(xs, mod[bp:n_c], norm_g[l], wts, attn_s, conv0_s, SAMPLE_PAD, n_new, h0)

        keep = min(WINDOW, sp)
        outs[0].append(kp[:, sp - keep:].astype(F32).reshape(bp, keep, *heads))
        outs[1].append(vp[:, sp - keep:].astype(F32).reshape(bp, keep, *heads))
        outs[2].append(cp)
        outs[3].append(hp.reshape(bp, *ssm_shape))
        outs[4].append(ks[:, :n_new].astype(F32).reshape(bs, n_new, *heads))
        outs[5].append(vs[:, :n_new].astype(F32).reshape(bs, n_new, *heads))
        outs[6].append(cs)
        outs[7].append(hs.reshape(bs, *ssm_shape))
    return (xp, xs[:, :n_new]) + tuple(jnp.stack(o) for o in outs)
```

```python
import functools

import jax
import jax.numpy as jnp
from jax import lax
from jax.experimental import pallas as pl
from jax.experimental.pallas import tpu as pltpu

F32 = jnp.float32
BF16 = jnp.bfloat16

D_MODEL = 1024
CHUNK = 64
BAND_CHUNKS = 8
WINDOW = BAND_CHUNKS * CHUNK
ATT_HEADS = 16
ATT_HEAD_DIM = 64
ATT_WIDTH = ATT_HEADS * ATT_HEAD_DIM
MAX_REL = 256
SSD_WIDTH = 2 * D_MODEL
SSD_HEAD_DIM = 64
SSD_HEADS = SSD_WIDTH // SSD_HEAD_DIM
SSD_GROUPS = 4
SSD_STATE = 128
CONV_WIDTH = 4
CONV_CH = SSD_WIDTH + 2 * SSD_GROUPS * SSD_STATE
PAST_LEN = 4096
EPS = 1e-6
NEG = -1e30
LOG2E = 1.4426950408889634

LANES = 128
SUBLANES = 8
VMEM_LIMIT = 56 * 1024 * 1024

IN_TM = 256
CONV_BLOCKS = 4
OUT_TM = 512
ATT_TQ = 256
ATT_TK = ATT_TQ + WINDOW
SSD_L = 256
SAMPLE_PAD = 128
HEADS_PER_GROUP = SSD_HEADS // SSD_GROUPS
GROUP_W = SSD_WIDTH // SSD_GROUPS
HALO = SUBLANES


def _const(shape, index_map):
    return pl.BlockSpec(shape, index_map, pipeline_mode=pl.Buffered(1))


def _params(n_axes):
    return pltpu.CompilerParams(dimension_semantics=("arbitrary",) * n_axes, vmem_limit_bytes=VMEM_LIMIT)


def _dot(a, b):
    return jnp.dot(a, b, preferred_element_type=F32)


def _dot_nt(a, b):
    return lax.dot_general(a, b, (((1,), (1,)), ((), ())), preferred_element_type=F32)


def _split3(x):
    hi = x.astype(BF16)
    r1 = x - hi.astype(F32)
    mid = r1.astype(BF16)
    lo = (r1 - mid.astype(F32)).astype(BF16)
    return hi, mid, lo


def _sigmoid(x):
    return 0.5 * jnp.tanh(0.5 * x) + 0.5


def _silu(x):
    h = 0.5 * x
    return h * jnp.tanh(h) + h


def _softplus(x):
    return jnp.maximum(x, 0.0) + jnp.log1p(jnp.exp(-jnp.abs(x)))


def _mod_kernel(c_ref, w_ref, b_ref, o_ref):
    c = c_ref[...]
    a = (c * jax.nn.sigmoid(c)).astype(BF16)
    o_ref[...] = _dot(a, w_ref[...].astype(BF16)) + b_ref[...]


def _modulation(c_all, w_ada, b_ada):
    n = c_all.shape[0]
    return pl.pallas_call(
        _mod_kernel,
        out_shape=jax.ShapeDtypeStruct((n, 3 * D_MODEL), F32),
        grid=(3,),
        in_specs=[
            pl.BlockSpec((n, D_MODEL), lambda j: (0, 0)),
            pl.BlockSpec((D_MODEL, D_MODEL), lambda j: (0, j)),
            pl.BlockSpec((1, D_MODEL), lambda j: (0, j)),
        ],
        out_specs=pl.BlockSpec((n, D_MODEL), lambda j: (0, j)),
        compiler_params=_params(1),
        name="adaln_modulation",
    )(c_all, w_ada, b_ada.reshape(1, -1))


def _pair_cols(mat, e0, rows, first):
    a = jnp.broadcast_to(mat[:, e0:e0 + 1], (rows, LANES))
    b = jnp.broadcast_to(mat[:, e0 + 1:e0 + 2], (rows, LANES))
    return jnp.where(first, a, b)


def _inproj_kernel(x_ref, shift_ref, scale_ref, ng_ref, conv0_ref, wq, wk, wv, wza, wzs, wxbc, wg, wdt, wdtT,
                   qg, kg, cw_ref, cb_ref, q_o, k_o, v_o, za_o, zs_o, xc_o, g_o, dt_o, dtT_o, tail_o, *xpads,
                   per_b, n_valid):
    tm = x_ref.shape[0]
    cblk = CONV_CH // len(xpads)
    lag = [CONV_WIDTH - 1 - j for j in range(CONV_WIDTH)]

    @pl.when(lax.rem(pl.program_id(0), per_b) == 0)
    def _():
        for i, xpad in enumerate(xpads):
            for j in range(CONV_WIDTH - 1):
                xpad[j, HALO:HALO + lag[j], :] = conv0_ref[HALO - lag[j]:HALO, i * cblk:(i + 1) * cblk]

    x = x_ref[...]
    ms = jnp.mean(x * x, axis=-1, keepdims=True)
    h = x * lax.rsqrt(ms + EPS) * ng_ref[...]
    h = h * (1.0 + scale_ref[...]) + shift_ref[...]
    hb = h.astype(BF16)
    first = lax.broadcasted_iota(jnp.int32, (1, LANES), 1) < ATT_HEAD_DIM

    def head_norm(y, g_ref):
        parts = []
        for p in range(ATT_WIDTH // LANES):
            yp = y[:, p * LANES:(p + 1) * LANES]
            sq = yp * yp
            head_a = jnp.sum(jnp.where(first, sq, 0.0), axis=-1, keepdims=True)
            head_b = jnp.sum(jnp.where(first, 0.0, sq), axis=-1, keepdims=True)
            r_a = lax.rsqrt(head_a * (1.0 / ATT_HEAD_DIM) + EPS)
            r_b = lax.rsqrt(head_b * (1.0 / ATT_HEAD_DIM) + EPS)
            parts.append(yp * jnp.where(first, r_a, r_b))
        return jnp.concatenate(parts, axis=1) * g_ref[...]

    def conv_matmul(i):
        raw = _dot(hb, wxbc[:, i * cblk:(i + 1) * cblk])
        for j in range(CONV_WIDTH):
            xpads[i][j, HALO + lag[j]:HALO + lag[j] + tm, :] = raw

    def conv_silu(i):
        xpad, cols = xpads[i], slice(i * cblk, (i + 1) * cblk)
        acc = cb_ref[:, cols]
        for j in range(CONV_WIDTH):
            acc = acc + xpad[j, HALO:HALO + tm, :] * cw_ref[j:j + 1, cols]
        xc_o[:, cols] = _silu(acc).astype(xc_o.dtype)
        tail_o[:, cols] = xpad[CONV_WIDTH - 1, n_valid:n_valid + HALO, :]
        for j in range(CONV_WIDTH - 1):
            xpad[j, HALO:HALO + lag[j], :] = xpad[j, HALO + tm:HALO + tm + lag[j], :]

    half = SSD_WIDTH // 2
    projections = [
        (lambda: _dot(hb, wv[...]), lambda y: v_o.__setitem__(..., y.astype(v_o.dtype))),
        (lambda: _dot(hb, wza[...]), lambda y: za_o.__setitem__(..., _silu(y).astype(za_o.dtype))),
    ]
    for c0 in (0, half):
        cols = slice(c0, c0 + half)
        projections.append((lambda cols=cols: _dot(hb, wzs[:, cols]),
                            lambda y, cols=cols: zs_o.__setitem__((slice(None), cols), _silu(y).astype(zs_o.dtype))))
    for c0 in (0, half):
        cols = slice(c0, c0 + half)
        projections.append((lambda cols=cols: _dot(hb, wg[:, cols]),
                            lambda y, cols=cols: g_o.__setitem__((slice(None), cols), _sigmoid(y).astype(g_o.dtype))))
    projections += [
        (lambda: _dot(hb, wq[...]), lambda y: q_o.__setitem__(..., head_norm(y, qg).astype(q_o.dtype))),
        (lambda: _dot(hb, wk[...]), lambda y: k_o.__setitem__(..., head_norm(y, kg).astype(k_o.dtype))),
    ]

    n_conv = len(xpads)
    pending = {}
    for step in range(max(n_conv, len(projections)) + 1):
        if step < n_conv:
            conv_matmul(step)
        if step < len(projections):
            pending[step] = projections[step][0]()
        if 0 <= step - 1 < n_conv:
            conv_silu(step - 1)
        if 0 <= step - 1 < len(projections):
            projections[step - 1][1](pending.pop(step - 1))
    dt_o[...] = _dot(hb, wdt[...])
    dtT_o[...] = _dot_nt(wdtT[...], hb)


def _input_projection(x, shift, scale, norm_g, conv0, wts, tm, n_valid):
    b, s, _ = x.shape
    t = b * s
    per_b = s // tm
    x2 = x.reshape(t, D_MODEL)
    row = lambda i: (i, 0)
    bat = lambda i: (i // per_b, 0, 0)
    zero = lambda i: (0, 0)
    w_names = ("wq", "wk", "wv", "wza", "wzs", "wxbc", "wg", "wdt", "wdtT", "qg", "kg", "conv_w", "conv_b")
    w_list = [wts[n] for n in w_names]
    widths = (ATT_WIDTH, ATT_WIDTH, ATT_WIDTH, ATT_WIDTH, SSD_WIDTH, CONV_CH, 2 * D_MODEL)
    out_shape = [jax.ShapeDtypeStruct((t, w), BF16) for w in widths]
    out_shape += [jax.ShapeDtypeStruct((t, LANES), F32), jax.ShapeDtypeStruct((LANES, t), F32),
                  jax.ShapeDtypeStruct((b, HALO, CONV_CH), F32)]
    out_specs = [pl.BlockSpec((tm, w), row) for w in widths]
    out_specs += [pl.BlockSpec((tm, LANES), row), pl.BlockSpec((LANES, tm), lambda i: (0, i)),
                  pl.BlockSpec((None, HALO, CONV_CH), bat)]
    return pl.pallas_call(
        functools.partial(_inproj_kernel, per_b=per_b, n_valid=n_valid),
        out_shape=out_shape,
        grid=(t // tm,),
        in_specs=[
            pl.BlockSpec((tm, D_MODEL), row),
            pl.BlockSpec((None, 1, D_MODEL), bat),
            pl.BlockSpec((None, 1, D_MODEL), bat),
            _const((1, D_MODEL), zero),
            pl.BlockSpec((None, HALO, CONV_CH), bat),
        ] + [_const(w.shape, zero) for w in w_list],
        out_specs=out_specs,
        scratch_shapes=[pltpu.VMEM((CONV_WIDTH, HALO + tm + HALO, CONV_CH // CONV_BLOCKS), F32)
                        for _ in range(CONV_BLOCKS)],
        compiler_params=_params(1),
        name="input_projection",
    )(x2, shift, scale, norm_g.reshape(1, -1), conv0, *w_list)


def _bias_kernel(rb_ref, bp_ref, bsc_ref, bsn_ref, *, n_new, past_len):
    tq, tk = ATT_TQ, ATT_TK
    w = tq + tk
    n_tab = rb_ref.shape[1]
    v = lax.broadcasted_iota(jnp.int32, (n_tab, w), 1)
    d = lax.broadcasted_iota(jnp.int32, (n_tab, w), 0)
    idx = jnp.clip(tq - v + WINDOW, -MAX_REL, MAX_REL) + MAX_REL
    onehot = jnp.where(idx == d, 1.0, 0.0).astype(BF16)
    hi, mid, lo = _split3(rb_ref[...])
    base = _dot(hi, onehot) + _dot(mid, onehot) + _dot(lo, onehot)

    chunk_of = lambda pos: jnp.right_shift(pos, CHUNK.bit_length() - 1)
    qi = chunk_of(lax.broadcasted_iota(jnp.int32, (tq, tk), 0))
    kj = chunk_of(lax.broadcasted_iota(jnp.int32, (tq, tk), 1))
    band = (kj >= qi) & (kj <= qi + BAND_CHUNKS)
    sq = chunk_of(past_len + lax.broadcasted_iota(jnp.int32, (n_new, tk), 0))
    sk = chunk_of(past_len - WINDOW + lax.broadcasted_iota(jnp.int32, (n_new, tk), 1))
    sband = (sk <= sq) & (sk >= sq - BAND_CHUNKS)
    for h in range(ATT_HEADS):
        rows = jnp.broadcast_to(base[h:h + 1, :], (tq, w))
        toep = pltpu.roll(rows, 0, 1, stride=1, stride_axis=0)[:, tq:]
        bp_ref[h] = jnp.where(band, toep * LOG2E, NEG)
        srow = jnp.where(sband, toep[:n_new, :], NEG)
        bsc_ref[h] = srow[:, :WINDOW]
        bsn_ref[h] = srow[:, WINDOW:WINDOW + n_new]


def _bias_tables(rel_bias, n_new, past_len):
    n_tab = 2 * MAX_REL + 1
    pad = (-n_tab) % LANES
    rb = jnp.pad(rel_bias, ((0, 0), (0, pad)))
    return pl.pallas_call(
        functools.partial(_bias_kernel, n_new=n_new, past_len=past_len),
        out_shape=[
            jax.ShapeDtypeStruct((ATT_HEADS, ATT_TQ, ATT_TK), F32),
            jax.ShapeDtypeStruct((ATT_HEADS, n_new, WINDOW), F32),
            jax.ShapeDtypeStruct((ATT_HEADS, n_new, n_new), F32),
        ],
        compiler_params=pltpu.CompilerParams(vmem_limit_bytes=VMEM_LIMIT),
        name="relative_bias_tables",
    )(rb)


def _softmax_pv(s_blocks, v_blocks):
    m = s_blocks[0].max(axis=-1, keepdims=True)
    for s in s_blocks[1:]:
        m = jnp.maximum(m, s.max(axis=-1, keepdims=True))
    den = None
    acc = None
    for s, vb in zip(s_blocks, v_blocks):
        e = jnp.exp(s - m)
        part = e.sum(axis=-1, keepdims=True)
        den = part if den is None else den + part
        pv = _dot(e.astype(BF16), vb)
        acc = pv if acc is None else acc + pv
    return acc * (1.0 / den)


def _attn_kernel(q_ref, k0, k1, k2, v0, v1, v2, z_ref, bias_ref, o_ref):
    tq = ATT_TQ
    t = pl.program_id(1)
    first = lax.broadcasted_iota(jnp.int32, (1, LANES), 1) < ATT_HEAD_DIM
    kblocks, vblocks = (k0, k1, k2), (v0, v1, v2)
    n_blocks = len(kblocks)
    n_masked = n_blocks - 1

    def body(masked):
        def stage_scores(h):
            p, a = divmod(h, 2)
            sl = slice(p * LANES, (p + 1) * LANES)
            q2 = q_ref[:, sl].astype(F32) * (ATT_HEAD_DIM ** -0.5 * LOG2E)
            qa = jnp.where(first if a == 0 else ~first, q2, 0.0).astype(BF16)
            return [_dot_nt(qa, kblocks[j][:, sl]) for j in range(n_blocks)]

        def stage_softmax(h, qk_blocks):
            half = tq // 2
            per_tile = LANES // CHUNK
            n_tiles = n_blocks * tq // LANES
            e_tiles, inv = [], []
            for r in range(2):
                rows = slice(r * half, (r + 1) * half)
                c_lo, c_hi = r * half // CHUNK, (r + 1) * half // CHUNK - 1
                tiles = {}
                for tile in range(n_tiles):
                    if tile * per_tile + per_tile - 1 < c_lo or tile * per_tile > c_hi + BAND_CHUNKS:
                        continue
                    j, lt = divmod(tile, tq // LANES)
                    s = qk_blocks[j][rows, lt * LANES:(lt + 1) * LANES] \
                        + bias_ref[h, rows, tile * LANES:(tile + 1) * LANES]
                    if masked and j < n_masked:
                        s = jnp.where(t >= n_masked - j, s, NEG)
                    tiles[tile] = s
                m = functools.reduce(jnp.maximum, tiles.values()).max(axis=-1, keepdims=True)
                exps = {tile: jnp.exp2(s - m) for tile, s in tiles.items()}
                den = functools.reduce(jnp.add, exps.values()).sum(axis=-1, keepdims=True)
                e_tiles.append({tile: e.astype(BF16) for tile, e in exps.items()})
                inv.append(1.0 / den)
            zeros = jnp.zeros((half, LANES), BF16)
            e_blocks = []
            for j in range(n_blocks):
                tile0 = j * (tq // LANES)
                e_blocks.append(jnp.concatenate(
                    [jnp.concatenate([e_tiles[r].get(tile0 + lt, zeros) for lt in range(tq // LANES)], axis=1)
                     for r in range(2)], axis=0))
            return e_blocks, jnp.concatenate(inv, axis=0)

        def stage_pv(h, e_blocks, inv):
            sl = slice((h // 2) * LANES, (h // 2 + 1) * LANES)
            acc = _dot(e_blocks[0], vblocks[0][:, sl])
            for j in range(1, n_blocks):
                acc = acc + _dot(e_blocks[j], vblocks[j][:, sl])
            return acc * inv

        scores, probs, outs = {}, {}, {}
        for n in range(ATT_HEADS + 2):
            if 0 <= n - 2 < ATT_HEADS:
                h = n - 2
                outs[h] = stage_pv(h, *probs.pop(h))
                if h % 2 == 1:
                    sl = slice((h // 2) * LANES, (h // 2 + 1) * LANES)
                    o2 = jnp.where(first, outs.pop(h - 1), outs.pop(h))
                    o_ref[:, sl] = (o2 * z_ref[:, sl].astype(F32)).astype(o_ref.dtype)
            if n < ATT_HEADS:
                scores[n] = stage_scores(n)
            if 0 <= n - 1 < ATT_HEADS:
                probs[n - 1] = stage_softmax(n - 1, scores.pop(n - 1))

    @pl.when(t < n_masked)
    def _():
        body(True)

    @pl.when(t >= n_masked)
    def _():
        body(False)


def _prompt_attention(q, k, v, za, bias, b, s):
    tq = ATT_TQ
    nt = s // tq
    cur = lambda i, t: (i * nt + t, 0)
    prev1 = lambda i, t: (i * nt + jnp.maximum(t - 1, 0), 0)
    prev2 = lambda i, t: (i * nt + jnp.maximum(t - 2, 0), 0)
    blk = lambda m: pl.BlockSpec((tq, ATT_WIDTH), m)
    return pl.pallas_call(
        _attn_kernel,
        out_shape=jax.ShapeDtypeStruct((b * s, ATT_WIDTH), BF16),
        grid=(b, nt),
        in_specs=[blk(cur), blk(prev2), blk(prev1), blk(cur), blk(prev2), blk(prev1), blk(cur), blk(cur),
                  _const((ATT_HEADS, tq, ATT_TK), lambda i, t: (0, 0, 0))],
        out_specs=blk(cur),
        compiler_params=_params(2),
        name="prompt_band_attention",
    )(q, k, k, k, v, v, v, za, bias)


def _sample_attn_kernel(q_ref, kc_ref, vc_ref, kn_ref, vn_ref, z_ref, bc_ref, bn_ref, o_ref):
    scale = ATT_HEAD_DIM ** -0.5
    n_new = q_ref.shape[0]
    o_ref[n_new:, :] = jnp.zeros((o_ref.shape[0] - n_new, o_ref.shape[1]), o_ref.dtype)
    first = lax.broadcasted_iota(jnp.int32, (1, LANES), 1) < ATT_HEAD_DIM
    for p in range(ATT_WIDTH // LANES):
        sl = slice(p * LANES, (p + 1) * LANES)
        q2 = q_ref[:, sl]
        kb = [kc_ref[:, sl].astype(BF16), kn_ref[:, sl]]
        vb = [vc_ref[:, sl].astype(BF16), vn_ref[:, sl]]
        zero = jnp.zeros_like(q2)
        qs = jnp.concatenate([jnp.where(first, q2, zero), jnp.where(first, zero, q2)], axis=0)
        bias_c = jnp.concatenate([bc_ref[2 * p], bc_ref[2 * p + 1]], axis=0)
        bias_n = jnp.concatenate([bn_ref[2 * p], bn_ref[2 * p + 1]], axis=0)
        s_blocks = [_dot_nt(qs, kb[0]) * scale + bias_c, _dot_nt(qs, kb[1]) * scale + bias_n]
        res = _softmax_pv(s_blocks, vb)
        o2 = jnp.where(first, res[:n_new], res[n_new:])
        o_ref[:n_new, sl] = (o2 * z_ref[:, sl].astype(F32)).astype(o_ref.dtype)


def _sample_attention(q, k, v, za, cache_k, cache_v, bias_c, bias_n, nb, n_new, s_pad):
    step = s_pad // n_new
    new = pl.BlockSpec((n_new, ATT_WIDTH), lambda i: (i * step, 0))
    old = pl.BlockSpec((None, WINDOW, ATT_WIDTH), lambda i: (i, 0, 0))
    return pl.pallas_call(
        _sample_attn_kernel,
        out_shape=jax.ShapeDtypeStruct((nb * s_pad, ATT_WIDTH), BF16),
        grid=(nb,),
        in_specs=[new, old, old, new, new, new,
                  _const(bias_c.shape, lambda i: (0, 0, 0)), _const(bias_n.shape, lambda i: (0, 0, 0))],
        out_specs=pl.BlockSpec((s_pad, ATT_WIDTH), lambda i: (i, 0)),
        compiler_params=_params(1),
        name="sample_band_attention",
    )(q, cache_k, cache_v, k, v, za, bias_c, bias_n)


def _ssd_kernel(*refs, chunk, n_valid, has_init):
    if has_init:
        (xc_ref, dt_ref, dtT_ref, dtb_ref, dtbT_ref, alog_ref, alogT_ref, dsk_ref, h0_ref,
         y_ref, hout_ref, xw, hT) = refs
    else:
        (xc_ref, dt_ref, dtT_ref, dtb_ref, dtbT_ref, alog_ref, alogT_ref, dsk_ref,
         y_ref, hout_ref, xw, hT) = refs
    L = chunk
    c = pl.program_id(1)
    n_state = SSD_GROUPS * SSD_STATE

    @pl.when(c == 0)
    def _():
        if has_init:
            for g in range(SSD_GROUPS):
                hT[g] = h0_ref[g * GROUP_W:(g + 1) * GROUP_W, :].T
        else:
            hT[...] = jnp.zeros(hT.shape, F32)

    dt = _softplus(dt_ref[...] + dtb_ref[...])
    dtT = _softplus(dtT_ref[...] + dtbT_ref[...])
    if n_valid < L:
        dt = jnp.where(lax.broadcasted_iota(jnp.int32, dt.shape, 0) < n_valid, dt, 0.0)
        dtT = jnp.where(lax.broadcasted_iota(jnp.int32, dtT.shape, 1) < n_valid, dtT, 0.0)
    da = dt * (-jnp.exp(alog_ref[...]))
    daT = dtT * (-jnp.exp(alogT_ref[...]))
    ri = lax.broadcasted_iota(jnp.int32, (L, L), 0)
    ci = lax.broadcasted_iota(jnp.int32, (L, L), 1)
    causal = ri >= ci
    lower = jnp.where(causal, 1.0, 0.0).astype(BF16)
    upper = jnp.where(ri <= ci, 1.0, 0.0).astype(BF16)
    acs = sum(_dot(lower, part) for part in _split3(da))
    acsT = sum(_dot(part, upper) for part in _split3(daT))
    acs_last = acs[L - 1:L, :]
    w_end = dt * jnp.exp(acs_last - acs)
    e_last = jnp.exp(acs_last)
    col_term = acs * LOG2E
    row_term = (acsT - jnp.log(dtT)) * LOG2E

    first = lax.broadcasted_iota(jnp.int32, (1, LANES), 1) < SSD_HEAD_DIM

    for g in range(SSD_GROUPS):
        b_g = xc_ref[:, SSD_WIDTH + g * SSD_STATE:SSD_WIDTH + (g + 1) * SSD_STATE]
        c_g = xc_ref[:, SSD_WIDTH + n_state + g * SSD_STATE:SSD_WIDTH + n_state + (g + 1) * SSD_STATE]
        cb = _dot_nt(c_g, b_g)
        h_old = hT[g]
        y_state = _dot(c_g, h_old.astype(BF16))
        decays = []
        for pp in range(HEADS_PER_GROUP // 2):
            e0 = g * HEADS_PER_GROUP + 2 * pp
            col0 = g * GROUP_W + pp * LANES
            xp_b = xc_ref[:, col0:col0 + LANES]
            xp = xp_b.astype(F32)
            cols = [jnp.broadcast_to(col_term[:, e0 + a:e0 + a + 1], (L, LANES)) for a in range(2)]
            y = y_state[:, pp * LANES:(pp + 1) * LANES] * jnp.exp2(jnp.where(first, cols[0], cols[1]))
            for a in range(2):
                e = e0 + a
                seg = jnp.concatenate([cols[a]] * (L // LANES), axis=1) - row_term[e:e + 1, :]
                m = cb * jnp.exp2(jnp.where(causal, seg, NEG))
                xa = jnp.where(first if a == 0 else ~first, xp_b, jnp.zeros_like(xp_b))
                y = y + _dot(m.astype(BF16), xa)
            y_ref[:, col0:col0 + LANES] = (y + dsk_ref[:, col0:col0 + LANES] * xp).astype(y_ref.dtype)
            xw[:, pp * LANES:(pp + 1) * LANES] = (xp * _pair_cols(w_end, e0, L, first)).astype(xw.dtype)
            decays.append(_pair_cols(e_last, e0, 1, first))
        b_gt = b_g.astype(F32).T.astype(BF16)
        hT[g] = h_old * jnp.concatenate(decays, axis=1) + _dot(b_gt, xw[...])

    @pl.when(c == pl.num_programs(1) - 1)
    def _():
        for g in range(SSD_GROUPS):
            hout_ref[g * GROUP_W:(g + 1) * GROUP_W, :] = hT[g].T


def _ssd(xc, dt, dtT, wts, b, s, chunk, n_valid, h0=None):
    nc = s // chunk
    has_init = h0 is not None
    row = lambda i, c: (i * nc + c, 0)
    zero = lambda i, c: (0, 0)
    per_b = lambda i, c: (i, 0, 0)
    names = ("dt_b", "dt_bT", "a_log", "a_logT", "d_skip")
    w_list = [wts[n] for n in names]
    in_specs = [
        pl.BlockSpec((chunk, CONV_CH), row),
        pl.BlockSpec((chunk, LANES), row),
        pl.BlockSpec((LANES, chunk), lambda i, c: (0, i * nc + c)),
    ] + [_const(w.shape, zero) for w in w_list]
    args = [xc, dt, dtT] + w_list
    if has_init:
        in_specs += [pl.BlockSpec((None, SSD_HEADS * SSD_HEAD_DIM, SSD_STATE), per_b)]
        args += [h0]
    return pl.pallas_call(
        functools.partial(_ssd_kernel, chunk=chunk, n_valid=n_valid, has_init=has_init),
        out_shape=[jax.ShapeDtypeStruct((b * s, SSD_WIDTH), BF16),
                   jax.ShapeDtypeStruct((b, SSD_HEADS * SSD_HEAD_DIM, SSD_STATE), F32)],
        grid=(b, nc),
        in_specs=in_specs,
        out_specs=[pl.BlockSpec((chunk, SSD_WIDTH), row),
                   pl.BlockSpec((None, SSD_HEADS * SSD_HEAD_DIM, SSD_STATE), per_b)],
        scratch_shapes=[
            pltpu.VMEM((chunk, GROUP_W), BF16),
            pltpu.VMEM((SSD_GROUPS, SSD_STATE, GROUP_W), F32),
        ],
        compiler_params=_params(2),
        name="ssd_scan",
    )(*args)


def _out_kernel(x_ref, gate_ref, att_ref, y_ref, zs_ref, g_ref, ng_ref, wap, wsp, wout, o_ref):
    att = _dot(att_ref[...], wap[...])
    ssd = None
    for g in range(SSD_GROUPS):
        cols = slice(g * GROUP_W, (g + 1) * GROUP_W)
        yg = y_ref[:, cols].astype(F32) * zs_ref[:, cols].astype(F32)
        ms = jnp.mean(yg * yg, axis=-1, keepdims=True)
        yn = (yg * lax.rsqrt(ms + EPS) * ng_ref[:, cols]).astype(BF16)
        part = _dot(yn, wsp[cols, :])
        ssd = part if ssd is None else ssd + part
    gates = g_ref[...].astype(F32)
    merged = gates[:, :D_MODEL] * att + gates[:, D_MODEL:] * ssd
    o_ref[...] = x_ref[...] + gate_ref[...] * _dot(merged.astype(BF16), wout[...])


def _output(x, gate, att, y, zs, gates, wts, tm):
    b, s, _ = x.shape
    t = b * s
    per_b = s // tm
    row = lambda i: (i, 0)
    zero = lambda i: (0, 0)
    w_list = [wts["ssd_ng"], wts["w_att_proj"], wts["w_ssd_proj"], wts["w_out"]]
    out = pl.pallas_call(
        _out_kernel,
        out_shape=jax.ShapeDtypeStruct((t, D_MODEL), F32),
        grid=(t // tm,),
        in_specs=[
            pl.BlockSpec((tm, D_MODEL), row),
            pl.BlockSpec((None, 1, D_MODEL), lambda i: (i // per_b, 0, 0)),
            pl.BlockSpec((tm, ATT_WIDTH), row),
            pl.BlockSpec((tm, SSD_WIDTH), row),
            pl.BlockSpec((tm, SSD_WIDTH), row),
            pl.BlockSpec((tm, 2 * D_MODEL), row),
        ] + [_const(w.shape, zero) for w in w_list],
        out_specs=pl.BlockSpec((tm, D_MODEL), row),
        compiler_params=_params(1),
        name="merge_output_projection",
    )(x.reshape(t, D_MODEL), gate, att, y, zs, gates, *w_list)
    return out.reshape(b, s, D_MODEL)


def _layer_weights(w_in, q_norm_g, k_norm_g, w_att_proj, conv_w, conv_b, dt_bias, a_log, d_skip, ssd_norm_g,
                   w_ssd_proj, w_out):
    sizes = (ATT_WIDTH, ATT_WIDTH, ATT_WIDTH, ATT_WIDTH, SSD_WIDTH, CONV_CH, SSD_HEADS, 2 * D_MODEL)
    offs = [0]
    for n in sizes:
        offs.append(offs[-1] + n)
    wq, wk, wv, wza, wzs, wxbc, wdt, wg = (w_in[:, offs[i]:offs[i + 1]].astype(BF16) for i in range(len(sizes)))
    pad_h = LANES - SSD_HEADS
    row_vec = lambda v: jnp.pad(v.astype(F32), (0, pad_h)).reshape(1, LANES)
    return {
        "wq": wq, "wk": wk, "wv": wv, "wza": wza, "wzs": wzs, "wxbc": wxbc, "wg": wg,
        "wdt": jnp.pad(wdt, ((0, 0), (0, pad_h))),
        "wdtT": jnp.pad(wdt.T, ((0, pad_h), (0, 0))),
        "qg": jnp.tile(q_norm_g.astype(F32), ATT_HEADS).reshape(1, ATT_WIDTH),
        "kg": jnp.tile(k_norm_g.astype(F32), ATT_HEADS).reshape(1, ATT_WIDTH),
        "conv_w": conv_w.astype(F32), "conv_b": conv_b.astype(F32).reshape(1, CONV_CH),
        "dt_b": row_vec(dt_bias), "dt_bT": row_vec(dt_bias).reshape(LANES, 1),
        "a_log": row_vec(a_log), "a_logT": row_vec(a_log).reshape(LANES, 1),
        "d_skip": jnp.repeat(d_skip.astype(F32), SSD_HEAD_DIM).reshape(1, SSD_WIDTH),
        "ssd_ng": ssd_norm_g.astype(F32).reshape(1, SSD_WIDTH),
        "w_att_proj": w_att_proj.astype(BF16), "w_ssd_proj": w_ssd_proj.astype(BF16), "w_out": w_out.astype(BF16),
    }


def _trunk_layer(x, mod, norm_g, wts, attn_fn, conv0, chunk, n_valid, h0=None):
    b, s, _ = x.shape
    shift, scale, gate = (mod[:, :, i * D_MODEL:(i + 1) * D_MODEL] for i in range(3))
    tm = min(IN_TM, s)
    q, k, v, za, zs, xc, gates, dt, dtT, tail = _input_projection(x, shift, scale, norm_g, conv0, wts, tm,
                                                                 min(n_valid, tm))
    att = attn_fn(q, k, v, za)
    y, h_new = _ssd(xc, dt, dtT, wts, b, s, chunk, n_valid, h0)
    out = _output(x, gate, att, y, zs, gates, wts, min(OUT_TM, s))
    return out, k.reshape(b, s, ATT_WIDTH), v.reshape(b, s, ATT_WIDTH), tail[:, HALO - (CONV_WIDTH - 1):], h_new


def kernel(x_prompt, x_sample, c_prompt, c_sample, cache_k, cache_v, state_conv, state_ssm, norm_g, w_ada, b_ada, w_in, q_norm_g, k_norm_g, rel_bias, w_att_proj, conv_w, conv_b, dt_bias, a_log, d_skip, ssd_norm_g, w_ssd_proj, w_out):
    depth = w_in.shape[0]
    bp, sp, _ = x_prompt.shape
    bs, n_new, _ = x_sample.shape
    rows = cache_k.shape[2]
    assert rows == WINDOW and n_new <= CHUNK and n_new % SUBLANES == 0 and sp % SSD_L == 0 and sp % ATT_TQ == 0
    heads = (ATT_HEADS, ATT_HEAD_DIM)
    ssm_shape = (SSD_HEADS, SSD_HEAD_DIM, SSD_STATE)
    n_c = bp + bs
    c_all = jnp.pad(jnp.concatenate([c_prompt, c_sample], axis=0), ((0, (-n_c) % SUBLANES), (0, 0)))
    xp = x_prompt
    xs = jnp.pad(x_sample, ((0, 0), (0, SAMPLE_PAD - n_new), (0, 0)))
    outs = [[] for _ in range(8)]
    for l in range(depth):
        wts = _layer_weights(w_in[l], q_norm_g[l], k_norm_g[l], w_att_proj[l], conv_w[l], conv_b[l], dt_bias[l],
                             a_log[l], d_skip[l], ssd_norm_g[l], w_ssd_proj[l], w_out[l])
        mod = _modulation(c_all, w_ada[l], b_ada[l])[:, None, :]
        bias_p, bias_c, bias_n = _bias_tables(rel_bias[l], n_new, PAST_LEN)

        attn_p = functools.partial(_prompt_attention, bias=bias_p, b=bp, s=sp)
        conv0_p = jnp.zeros((bp, HALO, CONV_CH), F32)
        xp, kp, vp, cp, hp = _trunk_layer(xp, mod[:bp], norm_g[l], wts, attn_p, conv0_p, SSD_L, SSD_L)

        attn_s = functools.partial(
            _sample_attention, cache_k=cache_k[l].reshape(bs, rows, ATT_WIDTH),
            cache_v=cache_v[l].reshape(bs, rows, ATT_WIDTH), bias_c=bias_c, bias_n=bias_n,
            nb=bs, n_new=n_new, s_pad=SAMPLE_PAD)
        conv0_s = jnp.pad(state_conv[l].astype(F32), ((0, 0), (HALO - (CONV_WIDTH - 1), 0), (0, 0)))
        h0 = state_ssm[l].reshape(bs, SSD_HEADS * SSD_HEAD_DIM, SSD_STATE)
        xs, ks, vs, cs, hs = _trunk_layer(xs, mod[bp:n_c], norm_g[l], wts, attn_s, conv0_s, SAMPLE_PAD, n_new, h0)

        keep = min(WINDOW, sp)
        outs[0].append(kp[:, sp - keep:].astype(F32).reshape(bp, keep, *heads))
        outs[1].append(vp[:, sp - keep:].astype(F32).reshape(bp, keep, *heads))
        outs[2].append(cp)
        outs[3].append(hp.reshape(bp, *ssm_shape))
        outs[4].append(ks[:, :n_new].astype(F32).reshape(bs, n_new, *heads))
        outs[5].append(vs[:, :n_new].astype(F32).reshape(bs, n_new, *heads))
        outs[6].append(cs)
        outs[7].append(hs.reshape(bs, *ssm_shape))
    return (xp, xs[:, :n_new]) + tuple(jnp.stack(o) for o in outs)
```

```python
import functools

import jax
import jax.numpy as jnp
from jax import lax
from jax.experimental import pallas as pl
from jax.experimental.pallas import tpu as pltpu

F32 = jnp.float32
BF16 = jnp.bfloat16

D_MODEL = 1024
CHUNK = 64
BAND_CHUNKS = 8
WINDOW = BAND_CHUNKS * CHUNK
ATT_HEADS = 16
ATT_HEAD_DIM = 64
ATT_WIDTH = ATT_HEADS * ATT_HEAD_DIM
MAX_REL = 256
SSD_WIDTH = 2 * D_MODEL
SSD_HEAD_DIM = 64
SSD_HEADS = SSD_WIDTH // SSD_HEAD_DIM
SSD_GROUPS = 4
SSD_STATE = 128
CONV_WIDTH = 4
CONV_CH = SSD_WIDTH + 2 * SSD_GROUPS * SSD_STATE
PAST_LEN = 4096
EPS = 1e-6
NEG = -1e30
LOG2E = 1.4426950408889634

LANES = 128
SUBLANES = 8
VMEM_LIMIT = 56 * 1024 * 1024
VMEM_MIB = {"modulation": 32,"attention": 32, "sample_attention": 24, "ssd": 20, "output": 40}

IN_TM = 256
CONV_BLOCKS = 4
OUT_TM = 512
ATT_TQ = 256
ATT_TK = ATT_TQ + WINDOW
SSD_L = 256
SAMPLE_PAD = 128
HEADS_PER_GROUP = SSD_HEADS // SSD_GROUPS
GROUP_W = SSD_WIDTH // SSD_GROUPS
HALO = SUBLANES


def _const(shape, index_map):
    return pl.BlockSpec(shape, index_map, pipeline_mode=pl.Buffered(1))


def _stream(shape, index_map):
    return pl.BlockSpec(shape, index_map)


def _params(n_axes, vmem_mib=None):
    limit = VMEM_LIMIT if vmem_mib is None else vmem_mib * 1024 * 1024
    return pltpu.CompilerParams(dimension_semantics=("arbitrary",) * n_axes, vmem_limit_bytes=limit)


def _dot(a, b):
    return jnp.dot(a, b, preferred_element_type=F32)


def _dot_nt(a, b):
    return lax.dot_general(a, b, (((1,), (1,)), ((), ())), preferred_element_type=F32)


def _split3(x):
    hi = x.astype(BF16)
    r1 = x - hi.astype(F32)
    mid = r1.astype(BF16)
    lo = (r1 - mid.astype(F32)).astype(BF16)
    return hi, mid, lo


def _sigmoid(x):
    return 0.5 * jnp.tanh(0.5 * x) + 0.5


def _silu(x):
    h = 0.5 * x
    return h * jnp.tanh(h) + h


def _softplus(x):
    return jnp.maximum(x, 0.0) + jnp.log1p(jnp.exp(-jnp.abs(x)))


def _mod_kernel(c_ref, w_ref, b_ref, o_ref):
    c = c_ref[...]
    a = (c * jax.nn.sigmoid(c)).astype(BF16)
    o_ref[...] = _dot(a, w_ref[...].astype(BF16)) + b_ref[...]


def _modulation(c_all, w_ada, b_ada):
    n = c_all.shape[0]
    return pl.pallas_call(
        _mod_kernel,
        out_shape=jax.ShapeDtypeStruct((n, 3 * D_MODEL), F32),
        grid=(3,),
        in_specs=[
            pl.BlockSpec((n, D_MODEL), lambda j: (0, 0)),
            pl.BlockSpec((D_MODEL, D_MODEL), lambda j: (0, j)),
            pl.BlockSpec((1, D_MODEL), lambda j: (0, j)),
        ],
        out_specs=pl.BlockSpec((n, D_MODEL), lambda j: (0, j)),
        compiler_params=_params(1, VMEM_MIB["modulation"]),
        name="adaln_modulation",
    )(c_all, w_ada, b_ada.reshape(1, -1))


def _pair_cols(mat, e0, rows, first):
    a = jnp.broadcast_to(mat[:, e0:e0 + 1], (rows, LANES))
    b = jnp.broadcast_to(mat[:, e0 + 1:e0 + 2], (rows, LANES))
    return jnp.where(first, a, b)


def _inproj_kernel(x_ref, shift_ref, scale_ref, ng_ref, conv0_ref, wq, wk, wv, wza, wzs, wxbc, wg, wdt, wdtT,
                   qg, kg, cw_ref, cb_ref, q_o, k_o, v_o, za_o, zs_o, xc_o, g_o, dt_o, dtT_o, tail_o, *xpads,
                   per_b, n_valid):
    tm = x_ref.shape[0]
    cblk = CONV_CH // len(xpads)
    lag = [CONV_WIDTH - 1 - j for j in range(CONV_WIDTH)]

    @pl.when(lax.rem(pl.program_id(0), per_b) == 0)
    def _():
        for i, xpad in enumerate(xpads):
            for j in range(CONV_WIDTH - 1):
                xpad[j, HALO:HALO + lag[j], :] = conv0_ref[HALO - lag[j]:HALO, i * cblk:(i + 1) * cblk]

    x = x_ref[...]
    ms = jnp.mean(x * x, axis=-1, keepdims=True)
    h = x * lax.rsqrt(ms + EPS) * ng_ref[...]
    h = h * (1.0 + scale_ref[...]) + shift_ref[...]
    hb = h.astype(BF16)
    first = lax.broadcasted_iota(jnp.int32, (1, LANES), 1) < ATT_HEAD_DIM

    def head_norm(y, g_ref):
        parts = []
        for p in range(ATT_WIDTH // LANES):
            yp = y[:, p * LANES:(p + 1) * LANES]
            sq = yp * yp
            head_a = jnp.sum(jnp.where(first, sq, 0.0), axis=-1, keepdims=True)
            head_b = jnp.sum(jnp.where(first, 0.0, sq), axis=-1, keepdims=True)
            r_a = lax.rsqrt(head_a * (1.0 / ATT_HEAD_DIM) + EPS)
            r_b = lax.rsqrt(head_b * (1.0 / ATT_HEAD_DIM) + EPS)
            parts.append(yp * jnp.where(first, r_a, r_b))
        return jnp.concatenate(parts, axis=1) * g_ref[...]

    def conv_matmul(i):
        raw = _dot(hb, wxbc[:, i * cblk:(i + 1) * cblk])
        for j in range(CONV_WIDTH):
            xpads[i][j, HALO + lag[j]:HALO + lag[j] + tm, :] = raw

    def conv_silu(i):
        xpad, cols = xpads[i], slice(i * cblk, (i + 1) * cblk)
        acc = cb_ref[:, cols]
        for j in range(CONV_WIDTH):
            acc = acc + xpad[j, HALO:HALO + tm, :] * cw_ref[j:j + 1, cols]
        xc_o[:, cols] = _silu(acc).astype(xc_o.dtype)
        tail_o[:, cols] = xpad[CONV_WIDTH - 1, n_valid:n_valid + HALO, :]
        for j in range(CONV_WIDTH - 1):
            xpad[j, HALO:HALO + lag[j], :] = xpad[j, HALO + tm:HALO + tm + lag[j], :]

    half = SSD_WIDTH // 2
    projections = [
        (lambda: _dot(hb, wv[...]), lambda y: v_o.__setitem__(..., y.astype(v_o.dtype))),
        (lambda: _dot(hb, wza[...]), lambda y: za_o.__setitem__(..., _silu(y).astype(za_o.dtype))),
    ]
    for c0 in (0, half):
        cols = slice(c0, c0 + half)
        projections.append((lambda cols=cols: _dot(hb, wzs[:, cols]),
                            lambda y, cols=cols: zs_o.__setitem__((slice(None), cols), _silu(y).astype(zs_o.dtype))))
    for c0 in (0, half):
        cols = slice(c0, c0 + half)
        projections.append((lambda cols=cols: _dot(hb, wg[:, cols]),
                            lambda y, cols=cols: g_o.__setitem__((slice(None), cols), _sigmoid(y).astype(g_o.dtype))))
    projections += [
        (lambda: _dot(hb, wq[...]), lambda y: q_o.__setitem__(..., head_norm(y, qg).astype(q_o.dtype))),
        (lambda: _dot(hb, wk[...]), lambda y: k_o.__setitem__(..., head_norm(y, kg).astype(k_o.dtype))),
    ]

    n_conv = len(xpads)
    pending = {}
    for step in range(max(n_conv, len(projections)) + 1):
        if step < n_conv:
            conv_matmul(step)
        if step < len(projections):
            pending[step] = projections[step][0]()
        if 0 <= step - 1 < n_conv:
            conv_silu(step - 1)
        if 0 <= step - 1 < len(projections):
            projections[step - 1][1](pending.pop(step - 1))
    dt_o[...] = _dot(hb, wdt[...])
    dtT_o[...] = _dot_nt(wdtT[...], hb)


def _input_projection(x, shift, scale, norm_g, conv0, wts, tm, n_valid):
    b, s, _ = x.shape
    t = b * s
    per_b = s // tm
    x2 = x.reshape(t, D_MODEL)
    row = lambda i: (i, 0)
    bat = lambda i: (i // per_b, 0, 0)
    zero = lambda i: (0, 0)
    w_names = ("wq", "wk", "wv", "wza", "wzs", "wxbc", "wg", "wdt", "wdtT", "qg", "kg", "conv_w", "conv_b")
    w_list = [wts[n] for n in w_names]
    widths = (ATT_WIDTH, ATT_WIDTH, ATT_WIDTH, ATT_WIDTH, SSD_WIDTH, CONV_CH, 2 * D_MODEL)
    out_shape = [jax.ShapeDtypeStruct((t, w), BF16) for w in widths]
    out_shape += [jax.ShapeDtypeStruct((t, LANES), F32), jax.ShapeDtypeStruct((LANES, t), F32),
                  jax.ShapeDtypeStruct((b, HALO, CONV_CH), F32)]
    out_specs = [pl.BlockSpec((tm, w), row) for w in widths]
    out_specs += [pl.BlockSpec((tm, LANES), row), pl.BlockSpec((LANES, tm), lambda i: (0, i)),
                  pl.BlockSpec((None, HALO, CONV_CH), bat)]
    return pl.pallas_call(
        functools.partial(_inproj_kernel, per_b=per_b, n_valid=n_valid),
        out_shape=out_shape,
        grid=(t // tm,),
        in_specs=[
            _stream((tm, D_MODEL), row),
            pl.BlockSpec((None, 1, D_MODEL), bat),
            pl.BlockSpec((None, 1, D_MODEL), bat),
            _const((1, D_MODEL), zero),
            pl.BlockSpec((None, HALO, CONV_CH), bat),
        ] + [_const(w.shape, zero) for w in w_list],
        out_specs=out_specs,
        scratch_shapes=[pltpu.VMEM((CONV_WIDTH, HALO + tm + HALO, CONV_CH // CONV_BLOCKS), F32)
                        for _ in range(CONV_BLOCKS)],
        compiler_params=_params(1),
        name="input_projection",
    )(x2, shift, scale, norm_g.reshape(1, -1), conv0, *w_list)


def _bias_kernel(rb_ref, bp_ref, bsc_ref, bsn_ref, *, n_new, past_len):
    tq, tk = ATT_TQ, ATT_TK
    w = tq + tk
    n_tab = rb_ref.shape[1]
    v = lax.broadcasted_iota(jnp.int32, (n_tab, w), 1)
    d = lax.broadcasted_iota(jnp.int32, (n_tab, w), 0)
    idx = jnp.clip(tq - v + WINDOW, -MAX_REL, MAX_REL) + MAX_REL
    onehot = jnp.where(idx == d, 1.0, 0.0).astype(BF16)
    hi, mid, lo = _split3(rb_ref[...])
    base = _dot(hi, onehot) + _dot(mid, onehot) + _dot(lo, onehot)

    chunk_of = lambda pos: jnp.right_shift(pos, CHUNK.bit_length() - 1)
    qi = chunk_of(lax.broadcasted_iota(jnp.int32, (tq, tk), 0))
    kj = chunk_of(lax.broadcasted_iota(jnp.int32, (tq, tk), 1))
    band = (kj >= qi) & (kj <= qi + BAND_CHUNKS)
    sq = chunk_of(past_len + lax.broadcasted_iota(jnp.int32, (n_new, tk), 0))
    sk = chunk_of(past_len - WINDOW + lax.broadcasted_iota(jnp.int32, (n_new, tk), 1))
    sband = (sk <= sq) & (sk >= sq - BAND_CHUNKS)
    for h in range(ATT_HEADS):
        rows = jnp.broadcast_to(base[h:h + 1, :], (tq, w))
        toep = pltpu.roll(rows, 0, 1, stride=1, stride_axis=0)[:, tq:]
        bp_ref[h] = jnp.where(band, toep * LOG2E, NEG)
        srow = jnp.where(sband, toep[:n_new, :], NEG)
        bsc_ref[h] = srow[:, :WINDOW]
        bsn_ref[h] = srow[:, WINDOW:WINDOW + n_new]


def _bias_tables(rel_bias, n_new, past_len):
    n_tab = 2 * MAX_REL + 1
    pad = (-n_tab) % LANES
    rb = jnp.pad(rel_bias, ((0, 0), (0, pad)))
    return pl.pallas_call(
        functools.partial(_bias_kernel, n_new=n_new, past_len=past_len),
        out_shape=[
            jax.ShapeDtypeStruct((ATT_HEADS, ATT_TQ, ATT_TK), F32),
            jax.ShapeDtypeStruct((ATT_HEADS, n_new, WINDOW), F32),
            jax.ShapeDtypeStruct((ATT_HEADS, n_new, n_new), F32),
        ],
        compiler_params=pltpu.CompilerParams(vmem_limit_bytes=VMEM_LIMIT),
        name="relative_bias_tables",
    )(rb)


def _softmax_pv(s_blocks, v_blocks):
    m = s_blocks[0].max(axis=-1, keepdims=True)
    for s in s_blocks[1:]:
        m = jnp.maximum(m, s.max(axis=-1, keepdims=True))
    den = None
    acc = None
    for s, vb in zip(s_blocks, v_blocks):
        e = jnp.exp(s - m)
        part = e.sum(axis=-1, keepdims=True)
        den = part if den is None else den + part
        pv = _dot(e.astype(BF16), vb)
        acc = pv if acc is None else acc + pv
    return acc * (1.0 / den)


def _attn_kernel(q_ref, k0, k1, k2, v0, v1, v2, z_ref, bias_ref, o_ref):
    tq = ATT_TQ
    t = pl.program_id(1)
    first = lax.broadcasted_iota(jnp.int32, (1, LANES), 1) < ATT_HEAD_DIM
    kblocks, vblocks = (k0, k1, k2), (v0, v1, v2)
    n_blocks = len(kblocks)
    n_masked = n_blocks - 1

    def body(masked):
        def stage_scores(h):
            p, a = divmod(h, 2)
            sl = slice(p * LANES, (p + 1) * LANES)
            q2 = q_ref[:, sl].astype(F32) * (ATT_HEAD_DIM ** -0.5 * LOG2E)
            qa = jnp.where(first if a == 0 else ~first, q2, 0.0).astype(BF16)
            return [_dot_nt(qa, kblocks[j][:, sl]) for j in range(n_blocks)]

        def stage_softmax(h, qk_blocks):
            half = tq // 2
            per_tile = LANES // CHUNK
            n_tiles = n_blocks * tq // LANES
            e_tiles, inv = [], []
            for r in range(2):
                rows = slice(r * half, (r + 1) * half)
                c_lo, c_hi = r * half // CHUNK, (r + 1) * half // CHUNK - 1
                tiles = {}
                for tile in range(n_tiles):
                    if tile * per_tile + per_tile - 1 < c_lo or tile * per_tile > c_hi + BAND_CHUNKS:
                        continue
                    j, lt = divmod(tile, tq // LANES)
                    s = qk_blocks[j][rows, lt * LANES:(lt + 1) * LANES] \
                        + bias_ref[h, rows, tile * LANES:(tile + 1) * LANES]
                    if masked and j < n_masked:
                        s = jnp.where(t >= n_masked - j, s, NEG)
                    tiles[tile] = s
                m = functools.reduce(jnp.maximum, tiles.values()).max(axis=-1, keepdims=True)
                exps = {tile: jnp.exp2(s - m) for tile, s in tiles.items()}
                den = functools.reduce(jnp.add, exps.values()).sum(axis=-1, keepdims=True)
                e_tiles.append({tile: e.astype(BF16) for tile, e in exps.items()})
                inv.append(1.0 / den)
            zeros = jnp.zeros((half, LANES), BF16)
            e_blocks = []
            for j in range(n_blocks):
                tile0 = j * (tq // LANES)
                e_blocks.append(jnp.concatenate(
                    [jnp.concatenate([e_tiles[r].get(tile0 + lt, zeros) for lt in range(tq // LANES)], axis=1)
                     for r in range(2)], axis=0))
            return e_blocks, jnp.concatenate(inv, axis=0)

        def stage_pv(h, e_blocks, inv):
            sl = slice((h // 2) * LANES, (h // 2 + 1) * LANES)
            acc = _dot(e_blocks[0], vblocks[0][:, sl])
            for j in range(1, n_blocks):
                acc = acc + _dot(e_blocks[j], vblocks[j][:, sl])
            return acc * inv

        scores, probs, outs = {}, {}, {}
        for n in range(ATT_HEADS + 2):
            if 0 <= n - 2 < ATT_HEADS:
                h = n - 2
                outs[h] = stage_pv(h, *probs.pop(h))
                if h % 2 == 1:
                    sl = slice((h // 2) * LANES, (h // 2 + 1) * LANES)
                    o2 = jnp.where(first, outs.pop(h - 1), outs.pop(h))
                    o_ref[:, sl] = (o2 * z_ref[:, sl].astype(F32)).astype(o_ref.dtype)
            if n < ATT_HEADS:
                scores[n] = stage_scores(n)
            if 0 <= n - 1 < ATT_HEADS:
                probs[n - 1] = stage_softmax(n - 1, scores.pop(n - 1))

    @pl.when(t < n_masked)
    def _():
        body(True)

    @pl.when(t >= n_masked)
    def _():
        body(False)


def _prompt_attention(q, k, v, za, bias, b, s):
    tq = ATT_TQ
    nt = s // tq
    cur = lambda i, t: (i * nt + t, 0)
    prev1 = lambda i, t: (i * nt + jnp.maximum(t - 1, 0), 0)
    prev2 = lambda i, t: (i * nt + jnp.maximum(t - 2, 0), 0)
    blk = lambda m: _stream((tq, ATT_WIDTH), m)
    return pl.pallas_call(
        _attn_kernel,
        out_shape=jax.ShapeDtypeStruct((b * s, ATT_WIDTH), BF16),
        grid=(b, nt),
        in_specs=[blk(cur), blk(prev2), blk(prev1), blk(cur), blk(prev2), blk(prev1), blk(cur), blk(cur),
                  _const((ATT_HEADS, tq, ATT_TK), lambda i, t: (0, 0, 0))],
        out_specs=pl.BlockSpec((tq, ATT_WIDTH), cur),
        compiler_params=_params(2, VMEM_MIB["attention"]),
        name="prompt_band_attention",
    )(q, k, k, k, v, v, v, za, bias)


def _sample_attn_kernel(q_ref, kc_ref, vc_ref, kn_ref, vn_ref, z_ref, bc_ref, bn_ref, o_ref):
    scale = ATT_HEAD_DIM ** -0.5
    n_new = q_ref.shape[0]
    o_ref[n_new:, :] = jnp.zeros((o_ref.shape[0] - n_new, o_ref.shape[1]), o_ref.dtype)
    first = lax.broadcasted_iota(jnp.int32, (1, LANES), 1) < ATT_HEAD_DIM
    for p in range(ATT_WIDTH // LANES):
        sl = slice(p * LANES, (p + 1) * LANES)
        q2 = q_ref[:, sl]
        kb = [kc_ref[:, sl].astype(BF16), kn_ref[:, sl]]
        vb = [vc_ref[:, sl].astype(BF16), vn_ref[:, sl]]
        zero = jnp.zeros_like(q2)
        qs = jnp.concatenate([jnp.where(first, q2, zero), jnp.where(first, zero, q2)], axis=0)
        bias_c = jnp.concatenate([bc_ref[2 * p], bc_ref[2 * p + 1]], axis=0)
        bias_n = jnp.concatenate([bn_ref[2 * p], bn_ref[2 * p + 1]], axis=0)
        s_blocks = [_dot_nt(qs, kb[0]) * scale + bias_c, _dot_nt(qs, kb[1]) * scale + bias_n]
        res = _softmax_pv(s_blocks, vb)
        o2 = jnp.where(first, res[:n_new], res[n_new:])
        o_ref[:n_new, sl] = (o2 * z_ref[:, sl].astype(F32)).astype(o_ref.dtype)


def _sample_attention(q, k, v, za, cache_k, cache_v, bias_c, bias_n, nb, n_new, s_pad):
    step = s_pad // n_new
    new = pl.BlockSpec((n_new, ATT_WIDTH), lambda i: (i * step, 0))
    old = pl.BlockSpec((None, WINDOW, ATT_WIDTH), lambda i: (i, 0, 0))
    return pl.pallas_call(
        _sample_attn_kernel,
        out_shape=jax.ShapeDtypeStruct((nb * s_pad, ATT_WIDTH), BF16),
        grid=(nb,),
        in_specs=[new, old, old, new, new, new,
                  _const(bias_c.shape, lambda i: (0, 0, 0)), _const(bias_n.shape, lambda i: (0, 0, 0))],
        out_specs=pl.BlockSpec((s_pad, ATT_WIDTH), lambda i: (i, 0)),
        compiler_params=_params(1, VMEM_MIB["sample_attention"]),
        name="sample_band_attention",
    )(q, cache_k, cache_v, k, v, za, bias_c, bias_n)


def _ssd_kernel(*refs, chunk, n_valid, has_init):
    if has_init:
        (xc_ref, dt_ref, dtT_ref, dtb_ref, dtbT_ref, alog_ref, alogT_ref, dsk_ref, h0_ref,
         y_ref, hout_ref, xw, hT) = refs
    else:
        (xc_ref, dt_ref, dtT_ref, dtb_ref, dtbT_ref, alog_ref, alogT_ref, dsk_ref,
         y_ref, hout_ref, xw, hT) = refs
    L = chunk
    c = pl.program_id(1)
    n_state = SSD_GROUPS * SSD_STATE

    @pl.when(c == 0)
    def _():
        if has_init:
            for g in range(SSD_GROUPS):
                hT[g] = h0_ref[g * GROUP_W:(g + 1) * GROUP_W, :].T
        else:
            hT[...] = jnp.zeros(hT.shape, F32)

    dt = _softplus(dt_ref[...] + dtb_ref[...])
    dtT = _softplus(dtT_ref[...] + dtbT_ref[...])
    if n_valid < L:
        dt = jnp.where(lax.broadcasted_iota(jnp.int32, dt.shape, 0) < n_valid, dt, 0.0)
        dtT = jnp.where(lax.broadcasted_iota(jnp.int32, dtT.shape, 1) < n_valid, dtT, 0.0)
    da = dt * (-jnp.exp(alog_ref[...]))
    daT = dtT * (-jnp.exp(alogT_ref[...]))
    ri = lax.broadcasted_iota(jnp.int32, (L, L), 0)
    ci = lax.broadcasted_iota(jnp.int32, (L, L), 1)
    causal = ri >= ci
    lower = jnp.where(causal, 1.0, 0.0).astype(BF16)
    upper = jnp.where(ri <= ci, 1.0, 0.0).astype(BF16)
    acs = sum(_dot(lower, part) for part in _split3(da))
    acsT = sum(_dot(part, upper) for part in _split3(daT))
    acs_last = acs[L - 1:L, :]
    w_end = dt * jnp.exp(acs_last - acs)
    e_last = jnp.exp(acs_last)
    col_term = acs * LOG2E
    row_term = (acsT - jnp.log(dtT)) * LOG2E

    first = lax.broadcasted_iota(jnp.int32, (1, LANES), 1) < SSD_HEAD_DIM

    for g in range(SSD_GROUPS):
        b_g = xc_ref[:, SSD_WIDTH + g * SSD_STATE:SSD_WIDTH + (g + 1) * SSD_STATE]
        c_g = xc_ref[:, SSD_WIDTH + n_state + g * SSD_STATE:SSD_WIDTH + n_state + (g + 1) * SSD_STATE]
        cb = _dot_nt(c_g, b_g)
        h_old = hT[g]
        y_state = _dot(c_g, h_old.astype(BF16))
        decays = []
        for pp in range(HEADS_PER_GROUP // 2):
            e0 = g * HEADS_PER_GROUP + 2 * pp
            col0 = g * GROUP_W + pp * LANES
            xp_b = xc_ref[:, col0:col0 + LANES]
            xp = xp_b.astype(F32)
            cols = [jnp.broadcast_to(col_term[:, e0 + a:e0 + a + 1], (L, LANES)) for a in range(2)]
            y = y_state[:, pp * LANES:(pp + 1) * LANES] * jnp.exp2(jnp.where(first, cols[0], cols[1]))
            for a in range(2):
                e = e0 + a
                seg = jnp.concatenate([cols[a]] * (L // LANES), axis=1) - row_term[e:e + 1, :]
                m = cb * jnp.exp2(jnp.where(causal, seg, NEG))
                xa = jnp.where(first if a == 0 else ~first, xp_b, jnp.zeros_like(xp_b))
                y = y + _dot(m.astype(BF16), xa)
            y_ref[:, col0:col0 + LANES] = (y + dsk_ref[:, col0:col0 + LANES] * xp).astype(y_ref.dtype)
            xw[:, pp * LANES:(pp + 1) * LANES] = (xp * _pair_cols(w_end, e0, L, first)).astype(xw.dtype)
            decays.append(_pair_cols(e_last, e0, 1, first))
        b_gt = b_g.astype(F32).T.astype(BF16)
        hT[g] = h_old * jnp.concatenate(decays, axis=1) + _dot(b_gt, xw[...])

    @pl.when(c == pl.num_programs(1) - 1)
    def _():
        for g in range(SSD_GROUPS):
            hout_ref[g * GROUP_W:(g + 1) * GROUP_W, :] = hT[g].T


def _ssd(xc, dt, dtT, wts, b, s, chunk, n_valid, h0=None):
    nc = s // chunk
    has_init = h0 is not None
    row = lambda i, c: (i * nc + c, 0)
    zero = lambda i, c: (0, 0)
    per_b = lambda i, c: (i, 0, 0)
    names = ("dt_b", "dt_bT", "a_log", "a_logT", "d_skip")
    w_list = [wts[n] for n in names]
    in_specs = [
        _stream((chunk, CONV_CH), row),
        pl.BlockSpec((chunk, LANES), row),
        pl.BlockSpec((LANES, chunk), lambda i, c: (0, i * nc + c)),
    ] + [_const(w.shape, zero) for w in w_list]
    args = [xc, dt, dtT] + w_list
    if has_init:
        in_specs += [pl.BlockSpec((None, SSD_HEADS * SSD_HEAD_DIM, SSD_STATE), per_b)]
        args += [h0]
    return pl.pallas_call(
        functools.partial(_ssd_kernel, chunk=chunk, n_valid=n_valid, has_init=has_init),
        out_shape=[jax.ShapeDtypeStruct((b * s, SSD_WIDTH), BF16),
                   jax.ShapeDtypeStruct((b, SSD_HEADS * SSD_HEAD_DIM, SSD_STATE), F32)],
        grid=(b, nc),
        in_specs=in_specs,
        out_specs=[pl.BlockSpec((chunk, SSD_WIDTH), row),
                   pl.BlockSpec((None, SSD_HEADS * SSD_HEAD_DIM, SSD_STATE), per_b)],
        scratch_shapes=[
            pltpu.VMEM((chunk, GROUP_W), BF16),
            pltpu.VMEM((SSD_GROUPS, SSD_STATE, GROUP_W), F32),
        ],
        compiler_params=_params(2, VMEM_MIB["ssd"]),
        name="ssd_scan",
    )(*args)


def _out_kernel(x_ref, gate_ref, att_ref, y_ref, zs_ref, g_ref, ng_ref, wap, wsp, wout, o_ref):
    att = _dot(att_ref[...], wap[...])
    ssd = None
    for g in range(SSD_GROUPS):
        cols = slice(g * GROUP_W, (g + 1) * GROUP_W)
        yg = y_ref[:, cols].astype(F32) * zs_ref[:, cols].astype(F32)
        ms = jnp.mean(yg * yg, axis=-1, keepdims=True)
        yn = (yg * lax.rsqrt(ms + EPS) * ng_ref[:, cols]).astype(BF16)
        part = _dot(yn, wsp[cols, :])
        ssd = part if ssd is None else ssd + part
    gates = g_ref[...].astype(F32)
    merged = gates[:, :D_MODEL] * att + gates[:, D_MODEL:] * ssd
    o_ref[...] = x_ref[...] + gate_ref[...] * _dot(merged.astype(BF16), wout[...])


def _output(x, gate, att, y, zs, gates, wts, tm):
    b, s, _ = x.shape
    t = b * s
    per_b = s // tm
    row = lambda i: (i, 0)
    zero = lambda i: (0, 0)
    w_list = [wts["ssd_ng"], wts["w_att_proj"], wts["w_ssd_proj"], wts["w_out"]]
    out = pl.pallas_call(
        _out_kernel,
        out_shape=jax.ShapeDtypeStruct((t, D_MODEL), F32),
        grid=(t // tm,),
        in_specs=[
            _stream((tm, D_MODEL), row),
            pl.BlockSpec((None, 1, D_MODEL), lambda i: (i // per_b, 0, 0)),
            _stream((tm, ATT_WIDTH), row),
            _stream((tm, SSD_WIDTH), row),
            _stream((tm, SSD_WIDTH), row),
            _stream((tm, 2 * D_MODEL), row),
        ] + [_const(w.shape, zero) for w in w_list],
        out_specs=pl.BlockSpec((tm, D_MODEL), row),
        compiler_params=_params(1, VMEM_MIB["output"]),
        name="merge_output_projection",
    )(x.reshape(t, D_MODEL), gate, att, y, zs, gates, *w_list)
    return out.reshape(b, s, D_MODEL)


def _layer_weights(w_in, q_norm_g, k_norm_g, w_att_proj, conv_w, conv_b, dt_bias, a_log, d_skip, ssd_norm_g,
                   w_ssd_proj, w_out):
    sizes = (ATT_WIDTH, ATT_WIDTH, ATT_WIDTH, ATT_WIDTH, SSD_WIDTH, CONV_CH, SSD_HEADS, 2 * D_MODEL)
    offs = [0]
    for n in sizes:
        offs.append(offs[-1] + n)
    wq, wk, wv, wza, wzs, wxbc, wdt, wg = (w_in[:, offs[i]:offs[i + 1]].astype(BF16) for i in range(len(sizes)))
    pad_h = LANES - SSD_HEADS
    row_vec = lambda v: jnp.pad(v.astype(F32), (0, pad_h)).reshape(1, LANES)
    return {
        "wq": wq, "wk": wk, "wv": wv, "wza": wza, "wzs": wzs, "wxbc": wxbc, "wg": wg,
        "wdt": jnp.pad(wdt, ((0, 0), (0, pad_h))),
        "wdtT": jnp.pad(wdt.T, ((0, pad_h), (0, 0))),
        "qg": jnp.tile(q_norm_g.astype(F32), ATT_HEADS).reshape(1, ATT_WIDTH),
        "kg": jnp.tile(k_norm_g.astype(F32), ATT_HEADS).reshape(1, ATT_WIDTH),
        "conv_w": conv_w.astype(F32), "conv_b": conv_b.astype(F32).reshape(1, CONV_CH),
        "dt_b": row_vec(dt_bias), "dt_bT": row_vec(dt_bias).reshape(LANES, 1),
        "a_log": row_vec(a_log), "a_logT": row_vec(a_log).reshape(LANES, 1),
        "d_skip": jnp.repeat(d_skip.astype(F32), SSD_HEAD_DIM).reshape(1, SSD_WIDTH),
        "ssd_ng": ssd_norm_g.astype(F32).reshape(1, SSD_WIDTH),
        "w_att_proj": w_att_proj.astype(BF16), "w_ssd_proj": w_ssd_proj.astype(BF16), "w_out": w_out.astype(BF16),
    }


def _trunk_layer(x, mod, norm_g, wts, attn_fn, conv0, chunk, n_valid, h0=None):
    b, s, _ = x.shape
    shift, scale, gate = (mod[:, :, i * D_MODEL:(i + 1) * D_MODEL] for i in range(3))
    tm = min(IN_TM, s)
    q, k, v, za, zs, xc, gates, dt, dtT, tail = _input_projection(x, shift, scale, norm_g, conv0, wts, tm,
                                                                 min(n_valid, tm))
    att = attn_fn(q, k, v, za)
    y, h_new = _ssd(xc, dt, dtT, wts, b, s, chunk, n_valid, h0)
    out = _output(x, gate, att, y, zs, gates, wts, min(OUT_TM, s))
    return out, k.reshape(b, s, ATT_WIDTH), v.reshape(b, s, ATT_WIDTH), tail[:, HALO - (CONV_WIDTH - 1):], h_new


def kernel(x_prompt, x_sample, c_prompt, c_sample, cache_k, cache_v, state_conv, state_ssm, norm_g, w_ada, b_ada, w_in, q_norm_g, k_norm_g, rel_bias, w_att_proj, conv_w, conv_b, dt_bias, a_log, d_skip, ssd_norm_g, w_ssd_proj, w_out):
    depth = w_in.shape[0]
    bp, sp, _ = x_prompt.shape
    bs, n_new, _ = x_sample.shape
    rows = cache_k.shape[2]
    assert rows == WINDOW and n_new <= CHUNK and n_new % SUBLANES == 0 and sp % SSD_L == 0 and sp % ATT_TQ == 0
    heads = (ATT_HEADS, ATT_HEAD_DIM)
    ssm_shape = (SSD_HEADS, SSD_HEAD_DIM, SSD_STATE)
    n_c = bp + bs
    c_all = jnp.pad(jnp.concatenate([c_prompt, c_sample], axis=0), ((0, (-n_c) % SUBLANES), (0, 0)))
    xp = x_prompt
    xs = jnp.pad(x_sample, ((0, 0), (0, SAMPLE_PAD - n_new), (0, 0)))
    outs = [[] for _ in range(8)]
    for l in range(depth):
        wts = _layer_weights(w_in[l], q_norm_g[l], k_norm_g[l], w_att_proj[l], conv_w[l], conv_b[l], dt_bias[l],
                             a_log[l], d_skip[l], ssd_norm_g[l], w_ssd_proj[l], w_out[l])
        mod = _modulation(c_all, w_ada[l], b_ada[l])[:, None, :]
        bias_p, bias_c, bias_n = _bias_tables(rel_bias[l], n_new, PAST_LEN)

        attn_p = functools.partial(_prompt_attention, bias=bias_p, b=bp, s=sp)
        conv0_p = jnp.zeros((bp, HALO, CONV_CH), F32)
        xp, kp, vp, cp, hp = _trunk_layer(xp, mod[:bp], norm_g[l], wts, attn_p, conv0_p, SSD_L, SSD_L)

        attn_s = functools.partial(
            _sample_attention, cache_k=cache_k[l].reshape(bs, rows, ATT_WIDTH),
            cache_v=cache_v[l].reshape(bs, rows, ATT_WIDTH), bias_c=bias_c, bias_n=bias_n,
            nb=bs, n_new=n_new, s_pad=SAMPLE_PAD)
        conv0_s = jnp.pad(state_conv[l].astype(F32), ((0, 0), (HALO - (CONV_WIDTH - 1), 0), (0, 0)))
        h0 = state_ssm[l].reshape(bs, SSD_HEADS * SSD_HEAD_DIM, SSD_STATE)
        xs, ks, vs, cs, hs = _trunk_layer(xs, mod[bp:n_c], norm_g[l], wts, attn_s, conv0_s, SAMPLE_PAD, n_new, h0)

        keep = min(WINDOW, sp)
        outs[0].append(kp[:, sp - keep:].astype(F32).reshape(bp, keep, *heads))
        outs[1].append(vp[:, sp - keep:].astype(F32).reshape(bp, keep, *heads))
        outs[2].append(cp)
        outs[3].append(hp.reshape(bp, *ssm_shape))
        outs[4].append(ks[:, :n_new].astype(F32).reshape(bs, n_new, *heads))
        outs[5].append(vs[:, :n_new].astype(F32).reshape(bs, n_new, *heads))
        outs[6].append(cs)
        outs[7].append(hs.reshape(bs, *ssm_shape))
    return (xp, xs[:, :n_new]) + tuple(jnp.stack(o) for o in outs)
```
